```python
import math
import jax
import jax.numpy as jnp
from jax import lax
import numpy as np

D_MODEL = 1024
BATCH = 4
SEQ = 8192
DEPTH = 4

MIX_WIDTH = D_MODEL
A_WIDTH = MIX_WIDTH // 2
A_HEADS = 4
A_HEAD_DIM = A_WIDTH // A_HEADS
CHUNK = 128
CONV_K = 5
B_WIDTH = MIX_WIDTH - A_WIDTH
B_HEAD_DIM = 64
B_Q_HEADS = B_WIDTH // B_HEAD_DIM
B_KV_HEADS = 2
Q_PER_KV = B_Q_HEADS // B_KV_HEADS
WINDOW = 128
BLOCK = 128
N_BUCKETS = 32
MAX_DISTANCE = 128
EPS = 1e-6
NEG_INF = -1e30
IN_COLS = (2 * A_WIDTH + A_WIDTH + A_WIDTH + A_WIDTH + 4 * A_HEADS
           + B_WIDTH + 2 * B_KV_HEADS * B_HEAD_DIM + B_WIDTH)

kernel_name = 'hybrid_mlstm_swa_parallel_heads'


def rms_norm(x, w):
    xf = x.astype(jnp.float32)
    y = xf * lax.rsqrt(jnp.mean(xf * xf, axis=-1, keepdims=True) + EPS)
    return (y * w.astype(jnp.float32)).astype(x.dtype)


def centred_depthwise_conv(u, w, b):
    ch = u.shape[-1]
    out = lax.conv_general_dilated(
        u, w[:, None, :].astype(u.dtype), window_strides=(1,),
        padding=[(CONV_K // 2, CONV_K // 2)],
        dimension_numbers=('NWC', 'WIO', 'NWC'), feature_group_count=ch)
    return out + b.astype(u.dtype)


def mlstm_one_direction(q, k, v, i_pre, f_pre):
    bsz, seqlen, nh, dh = q.shape
    nc = seqlen // CHUNK

    def chunks(t):
        return t.reshape(bsz, nc, CHUNK, nh, -1).transpose(1, 0, 3, 2, 4)

    def gchunks(t):
        return t.reshape(bsz, nc, CHUNK, nh).transpose(1, 0, 3, 2)

    qc, kc, vc = chunks(q), chunks(k) * (dh ** -0.5), chunks(v)
    lic = gchunks(i_pre)
    lfc = gchunks(jax.nn.log_sigmoid(f_pre))
    tril = jnp.tril(jnp.ones((CHUNK, CHUNK), dtype=bool))

    def step(carry, inp):
        C, n, m = carry
        qt, kt, vt, li, lf = inp
        b = jnp.cumsum(lf, axis=-1)
        dmat = jnp.where(tril, b[..., :, None] - b[..., None, :] + li[..., None, :], -jnp.inf)
        inter = m[..., None] + b
        m_t = jnp.maximum(inter, jnp.max(dmat, axis=-1))
        s = jnp.einsum('bhtd,bhsd->bhts', qt, kt) * jnp.exp(dmat - m_t[..., None])
        inter_w = jnp.exp(inter - m_t)
        num = (jnp.einsum('bhts,bhsd->bhtd', s, vt)
               + inter_w[..., None] * jnp.einsum('bhtk,bhkv->bhtv', qt, C))
        den = jnp.sum(s, axis=-1) + inter_w * jnp.einsum('bhtk,bhk->bht', qt, n)
        h = num / jnp.maximum(jnp.abs(den), jnp.exp(-m_t))[..., None]
        b_last = b[..., -1]
        w_log = b_last[..., None] - b + li
        m_new = jnp.maximum(m + b_last, jnp.max(w_log, axis=-1))
        decay = jnp.exp(m + b_last - m_new)
        w = jnp.exp(w_log - m_new[..., None])
        C = decay[..., None, None] * C + jnp.einsum('bhs,bhsk,bhsv->bhkv', w, kt, vt)
        n = decay[..., None] * n + jnp.einsum('bhs,bhsk->bhk', w, kt)
        return (C, n, m_new), h

    init = (jnp.zeros((bsz, nh, dh, dh), jnp.float32),
            jnp.zeros((bsz, nh, dh), jnp.float32),
            jnp.zeros((bsz, nh), jnp.float32))
    _, hs = lax.scan(step, init, (qc, kc, vc, lic, lfc))
    return hs.transpose(1, 0, 3, 2, 4).reshape(bsz, seqlen, nh, dh)


def t5_bucket(rel):
    nb = N_BUCKETS // 2
    max_exact = nb // 2
    ret = jnp.where(rel > 0, nb, 0)
    n = jnp.abs(rel)
    nf = jnp.maximum(n, 1).astype(jnp.float32)
    large = max_exact + (jnp.log(nf / max_exact) / math.log(MAX_DISTANCE / max_exact)
                         * (nb - max_exact)).astype(jnp.int32)
    large = jnp.minimum(large, nb - 1)
    return ret + jnp.where(n < max_exact, n, large)


def windowed_gqa(q, k, v, sink, rel_bias):
    bsz, seqlen = q.shape[:2]
    nb = seqlen // BLOCK
    qb = q.reshape(bsz, nb, BLOCK, B_KV_HEADS, Q_PER_KV, B_HEAD_DIM)

    def band(t):
        tb = t.reshape(bsz, nb, BLOCK, B_KV_HEADS, B_HEAD_DIM)
        tp = jnp.pad(tb, ((0, 0), (1, 1), (0, 0), (0, 0), (0, 0)))
        return jnp.concatenate([tp[:, :-2], tp[:, 1:-1], tp[:, 2:]], axis=2)

    kw, vw = band(k), band(v)
    scores = jnp.einsum('bnqhgd,bnkhd->bnhgqk', qb, kw) * (B_HEAD_DIM ** -0.5)
    q_off = jnp.arange(BLOCK)
    k_off = jnp.arange(3 * BLOCK) - BLOCK
    rel = k_off[None, :] - q_off[:, None]
    bias = rel_bias.astype(jnp.float32)[t5_bucket(rel)]
    bias = bias.transpose(2, 0, 1).reshape(B_KV_HEADS, Q_PER_KV, BLOCK, 3 * BLOCK)
    key_pos = jnp.arange(nb)[:, None] * BLOCK + k_off[None, :]
    key_ok = (key_pos >= 0) & (key_pos < seqlen)
    mask = (jnp.abs(rel) <= WINDOW)[None, :, :] & key_ok[:, None, :]
    scores = jnp.where(mask[None, :, None, None], scores + bias, NEG_INF)
    sink_l = sink.astype(jnp.float32).reshape(B_KV_HEADS, Q_PER_KV)[:, :, None]
    m = jnp.maximum(jnp.max(scores, axis=-1), sink_l)
    p = jnp.exp(scores - m[..., None])
    denom = jnp.sum(p, axis=-1) + jnp.exp(sink_l - m)
    out = jnp.einsum('bnhgqk,bnkhd->bnqhgd', p, vw) / denom.transpose(0, 1, 4, 2, 3)[..., None]
    return out.reshape(bsz, seqlen, B_Q_HEADS * B_HEAD_DIM)


def hybrid_layer(x, norm_w, w_in, conv_w, conv_b, gate_b, mhn_w, sink, rel_bias, w_out):
    bsz, seqlen, _ = x.shape
    hn = rms_norm(x, norm_w)
    proj = hn @ w_in.astype(hn.dtype)
    sizes = [2 * A_WIDTH, A_WIDTH, A_WIDTH, A_WIDTH, 4 * A_HEADS,
             B_WIDTH, B_KV_HEADS * B_HEAD_DIM, B_KV_HEADS * B_HEAD_DIM, B_WIDTH]
    cuts = [int(c) for c in np.cumsum(sizes)[:-1]]
    qk_a, v_a, o_a, z_a, g_a, q_b, k_b, v_b, z_b = jnp.split(proj, cuts, axis=-1)

    qk_a = jax.nn.silu(centred_depthwise_conv(qk_a, conv_w, conv_b))
    q_a, k_a = jnp.split(qk_a.astype(jnp.float32), 2, axis=-1)
    hs = (bsz, seqlen, A_HEADS, A_HEAD_DIM)
    q_a, k_a = q_a.reshape(hs), k_a.reshape(hs)
    v_a = v_a.astype(jnp.float32).reshape(hs)
    gates = g_a.astype(jnp.float32) + gate_b.astype(jnp.float32)
    i_f, i_b, f_f, f_b = jnp.split(gates, 4, axis=-1)
    h_fwd = mlstm_one_direction(q_a, k_a, v_a, i_f, f_f)
    h_bwd = jnp.flip(mlstm_one_direction(jnp.flip(q_a, 1), jnp.flip(k_a, 1), jnp.flip(v_a, 1),
                                         jnp.flip(i_b, 1), jnp.flip(f_b, 1)), 1)
    h = jax.nn.sigmoid(o_a.astype(jnp.float32)).reshape(hs) * (h_fwd + h_bwd)
    h = h * lax.rsqrt(jnp.mean(h * h, axis=-1, keepdims=True) + EPS)
    h = h * mhn_w.astype(jnp.float32).reshape(A_HEADS, A_HEAD_DIM)
    y_a = h.reshape(bsz, seqlen, A_WIDTH) * jax.nn.silu(z_a.astype(jnp.float32))

    qh = q_b.astype(jnp.float32).reshape(bsz, seqlen, B_Q_HEADS, B_HEAD_DIM)
    kh = k_b.astype(jnp.float32).reshape(bsz, seqlen, B_KV_HEADS, B_HEAD_DIM)
    vh = v_b.astype(jnp.float32).reshape(bsz, seqlen, B_KV_HEADS, B_HEAD_DIM)
    y_b = windowed_gqa(qh, kh, vh, sink, rel_bias) * jax.nn.silu(z_b.astype(jnp.float32))

    y = jnp.concatenate([y_a, y_b], axis=-1).astype(x.dtype)
    return x + y @ w_out.astype(x.dtype)


def setup_inputs(seed: int = 0) -> dict:
    key = jax.random.key(seed)
    ks = jax.random.split(key, 12)
    f32 = jnp.float32
    x = jax.random.normal(ks[0], (BATCH, SEQ, D_MODEL), f32)
    norm_w = 1.0 + 0.02 * jax.random.normal(ks[1], (DEPTH, D_MODEL), f32)
    w_in = jax.random.normal(ks[2], (DEPTH, D_MODEL, IN_COLS), f32) * (D_MODEL ** -0.5)
    conv_w = jax.random.normal(ks[3], (DEPTH, CONV_K, 2 * A_WIDTH), f32) * (CONV_K ** -0.5)
    conv_b = 0.02 * jax.random.normal(ks[4], (DEPTH, 2 * A_WIDTH), f32)
    ig_b = 0.1 * jax.random.normal(ks[5], (DEPTH, 2 * A_HEADS), f32)
    fg_b = (jnp.tile(jnp.linspace(3.0, 6.0, A_HEADS, dtype=f32), 2)[None, :]
            + 0.1 * jax.random.normal(ks[6], (DEPTH, 2 * A_HEADS), f32))
    gate_b = jnp.concatenate([ig_b, fg_b], axis=-1)
    mhn_w = 1.0 + 0.02 * jax.random.normal(ks[7], (DEPTH, A_WIDTH), f32)
    sink = 0.5 * jax.random.normal(ks[8], (DEPTH, B_Q_HEADS), f32)
    rel_bias = 0.5 * jax.random.normal(ks[9], (N_BUCKETS, B_Q_HEADS), f32)
    w_out = jax.random.normal(ks[10], (DEPTH, MIX_WIDTH, D_MODEL), f32) * (MIX_WIDTH ** -0.5)
    final_norm_w = 1.0 + 0.02 * jax.random.normal(ks[11], (D_MODEL,), f32)
    return {'x': x, 'norm_w': norm_w, 'w_in': w_in, 'conv_w': conv_w, 'conv_b': conv_b,
            'gate_b': gate_b, 'mhn_w': mhn_w, 'sink': sink, 'rel_bias': rel_bias,
            'w_out': w_out, 'final_norm_w': final_norm_w}


def reference(x, norm_w, w_in, conv_w, conv_b, gate_b, mhn_w, sink, rel_bias, w_out, final_norm_w):
    for layer in range(DEPTH):
        x = hybrid_layer(x, norm_w[layer], w_in[layer], conv_w[layer], conv_b[layer],
                         gate_b[layer], mhn_w[layer], sink[layer], rel_bias, w_out[layer])
    return rms_norm(x, final_norm_w)
```

```python
import functools
import math

import jax
import jax.numpy as jnp
import numpy as np
from jax import lax
from jax.experimental import pallas as pl
from jax.experimental.pallas import tpu as pltpu

D_MODEL = 1024
A_WIDTH = 512
A_HEADS = 4
A_HEAD_DIM = 128
CHUNK = 128
CONV_K = 5
B_WIDTH = 512
B_HEAD_DIM = 64
B_Q_HEADS = 8
B_KV_HEADS = 2
WINDOW = 128
BLOCK = 128
N_BUCKETS = 32
MAX_DISTANCE = 128
EPS = 1e-6
NEG_INF = -1e30
N_GATES = 4 * A_HEADS
N_CHAN = 2 * A_HEADS

LANES = 128
SUBLANES = 8
VMEM_LIMIT_BYTES = 56 * 1024 * 1024

HALO = SUBLANES
BF16 = jnp.bfloat16
F32 = jnp.float32


def _params(*sem):
    return pltpu.CompilerParams(dimension_semantics=sem, vmem_limit_bytes=VMEM_LIMIT_BYTES)


def _dot(a, b):
    return jnp.dot(a, b, preferred_element_type=F32)


def _dot_nt(a, b):
    return lax.dot_general(a, b, (((1,), (1,)), ((), ())), preferred_element_type=F32)


def _log_sigmoid(x):
    return -(jnp.maximum(-x, 0.0) + jnp.log1p(jnp.exp(-jnp.abs(x))))


def _silu(x):
    return x * jax.nn.sigmoid(x)


def _in_proj_kernel(x_ref, xp_ref, xn_ref, nw_ref, wqk_ref, wa_ref, wb_ref, wgt_ref,
                    cw_ref, cb_ref, gb_ref,
                    q_ref, kt_ref, va_ref, og_ref, zg_ref, qb_ref, kb2_ref, vb2_ref, zb_ref, gr_ref,
                    u_scr, *, tm):
    i = pl.program_id(1)
    last = pl.num_programs(1) - 1
    nw = nw_ref[...]

    def norm(xv):
        y = xv * lax.rsqrt(jnp.mean(xv * xv, axis=-1, keepdims=True) + EPS)
        return (y * nw).astype(BF16)

    hn = norm(x_ref[0])
    hp = norm(xp_ref[0])
    hx = norm(xn_ref[0])
    wqk = wqk_ref[...]
    u_scr[HALO:HALO + tm, :] = _dot(hn, wqk)
    u_scr[0:HALO, :] = jnp.where(i > 0, _dot(hp, wqk), 0.0)
    u_scr[HALO + tm:2 * HALO + tm, :] = jnp.where(i < last, _dot(hx, wqk), 0.0)
    acc = u_scr[HALO - 2:HALO - 2 + tm, :] * cw_ref[0:1, :] + cb_ref[...]
    for t in range(1, CONV_K):
        acc = acc + u_scr[HALO - 2 + t:HALO - 2 + t + tm, :] * cw_ref[t:t + 1, :]
    qk = _silu(acc)
    q_ref[0] = qk[:, :A_WIDTH].astype(BF16)
    k = qk[:, A_WIDTH:] * (A_HEAD_DIM ** -0.5)
    kt_ref[0] = k.T.astype(BF16)

    a = _dot(hn, wa_ref[...])
    va_ref[0] = a[:, :A_WIDTH].astype(BF16)
    og_ref[0] = jax.nn.sigmoid(a[:, A_WIDTH:2 * A_WIDTH]).astype(BF16)
    zg_ref[0] = _silu(a[:, 2 * A_WIDTH:]).astype(BF16)

    bq = _dot(hn, wb_ref[:, :B_WIDTH])
    qb_ref[0] = (bq * (B_HEAD_DIM ** -0.5)).astype(BF16)
    kv = _dot(hn, wb_ref[:, B_WIDTH:B_WIDTH + 2 * LANES])
    half = lax.broadcasted_iota(jnp.int32, (tm, LANES), 1) < B_HEAD_DIM
    for src, dst in ((kv[:, :LANES], kb2_ref), (kv[:, LANES:], vb2_ref)):
        sw = pltpu.roll(src, B_HEAD_DIM, 1)
        dst[0, :, :LANES] = jnp.where(half, src, sw).astype(BF16)
        dst[0, :, LANES:] = jnp.where(half, sw, src).astype(BF16)
    zb_ref[0] = _silu(_dot(hn, wb_ref[:, B_WIDTH + 2 * LANES:])).astype(BF16)

    g = _dot_nt(wgt_ref[...], hn) + gb_ref[...]
    row = lax.broadcasted_iota(jnp.int32, g.shape, 0)
    gr_ref[0] = jnp.where(row < N_CHAN, g, _log_sigmoid(g))


def _in_proj(x, nw, wqk, wa, wb, wgt, cw, cb, gb, *, tm):
    bsz, seqlen, _ = x.shape
    nt = seqlen // tm
    hb = tm // HALO
    nhb = seqlen // HALO

    def full(arr):
        return pl.BlockSpec(arr.shape, lambda b, i: (0,) * arr.ndim)

    def rows(width):
        return pl.BlockSpec((1, tm, width), lambda b, i: (b, i, 0))

    out_shape = (
        jax.ShapeDtypeStruct((bsz, seqlen, A_WIDTH), BF16),
        jax.ShapeDtypeStruct((bsz, A_WIDTH, seqlen), BF16),
        jax.ShapeDtypeStruct((bsz, seqlen, A_WIDTH), BF16),
        jax.ShapeDtypeStruct((bsz, seqlen, A_WIDTH), BF16),
        jax.ShapeDtypeStruct((bsz, seqlen, A_WIDTH), BF16),
        jax.ShapeDtypeStruct((bsz, seqlen, B_WIDTH), BF16),
        jax.ShapeDtypeStruct((bsz, seqlen, 2 * LANES), BF16),
        jax.ShapeDtypeStruct((bsz, seqlen, 2 * LANES), BF16),
        jax.ShapeDtypeStruct((bsz, seqlen, B_WIDTH), BF16),
        jax.ShapeDtypeStruct((bsz, N_GATES, seqlen), F32),
    )
    out_specs = (
        rows(A_WIDTH),
        pl.BlockSpec((1, A_WIDTH, tm), lambda b, i: (b, 0, i)),
        rows(A_WIDTH), rows(A_WIDTH), rows(A_WIDTH), rows(B_WIDTH),
        rows(2 * LANES), rows(2 * LANES), rows(B_WIDTH),
        pl.BlockSpec((1, N_GATES, tm), lambda b, i: (b, 0, i)),
    )
    in_specs = [
        rows(D_MODEL),
        pl.BlockSpec((1, HALO, D_MODEL), lambda b, i: (b, jnp.maximum(i * hb - 1, 0), 0)),
        pl.BlockSpec((1, HALO, D_MODEL), lambda b, i: (b, jnp.minimum((i + 1) * hb, nhb - 1), 0)),
        full(nw), full(wqk), full(wa), full(wb), full(wgt), full(cw), full(cb), full(gb),
    ]
    return pl.pallas_call(
        functools.partial(_in_proj_kernel, tm=tm),
        grid=(bsz, nt),
        in_specs=in_specs,
        out_specs=out_specs,
        out_shape=out_shape,
        scratch_shapes=[pltpu.VMEM((tm + 2 * HALO, 2 * A_WIDTH), F32)],
        compiler_params=_params("parallel", "arbitrary"),
        name="in_proj",
    )(x, x, x, nw, wqk, wa, wb, wgt, cw, cb, gb)


def _gate_prep_kernel(gr_ref, ar_ref, br_ref, cc_ref, bc_ref, *, n_chunks):
    ti = lax.broadcasted_iota(jnp.int32, (CHUNK, CHUNK), 0)
    si = lax.broadcasted_iota(jnp.int32, (CHUNK, CHUNK), 1)
    upper = (ti <= si)
    lower = (ti >= si)
    diag = (ti == si)
    upper_f = upper.astype(F32)
    lower_f = lower.astype(F32)
    chan_row = lax.broadcasted_iota(jnp.int32, (N_CHAN, CHUNK), 0)
    chan_col = lax.broadcasted_iota(jnp.int32, (CHUNK, N_CHAN), 1)
    for c in range(n_chunks):
        sl = slice(c * CHUNK, (c + 1) * CHUNK)
        li = gr_ref[0, :N_CHAN, sl]
        lf = gr_ref[0, N_CHAN:, sl]
        b_pre = jnp.dot(lf, upper_f, preferred_element_type=F32, precision=lax.Precision.HIGHEST)
        b_suf = jnp.dot(lf, lower_f, preferred_element_type=F32, precision=lax.Precision.HIGHEST)
        b = jnp.where(chan_row < A_HEADS, b_pre, b_suf)
        a = li - b
        ar_ref[0, :, sl] = a
        br_ref[0, :, sl] = b
        cc = jnp.zeros((CHUNK, N_CHAN), F32)
        bc = jnp.zeros((CHUNK, N_CHAN), F32)
        for ch in range(N_CHAN):
            vis = lower if ch < A_HEADS else upper
            a_b = jnp.broadcast_to(a[ch:ch + 1, :], (CHUNK, CHUNK))
            b_b = jnp.broadcast_to(b[ch:ch + 1, :], (CHUNK, CHUNK))
            cm = jnp.max(jnp.where(vis, a_b, -jnp.inf), axis=1, keepdims=True)
            bt = jnp.sum(jnp.where(diag, b_b, 0.0), axis=1, keepdims=True)
            cc = jnp.where(chan_col == ch, cm, cc)
            bc = jnp.where(chan_col == ch, bt, bc)
        cc_ref[0, sl, :] = cc
        bc_ref[0, sl, :] = bc


def _gate_prep(gr, *, tg):
    bsz, _, seqlen = gr.shape
    row_spec = pl.BlockSpec((1, N_CHAN, tg), lambda b, i: (b, 0, i))
    col_spec = pl.BlockSpec((1, tg, N_CHAN), lambda b, i: (b, i, 0))
    return pl.pallas_call(
        functools.partial(_gate_prep_kernel, n_chunks=tg // CHUNK),
        grid=(bsz, seqlen // tg),
        in_specs=[pl.BlockSpec((1, N_GATES, tg), lambda b, i: (b, 0, i))],
        out_specs=(row_spec, row_spec, col_spec, col_spec),
        out_shape=(jax.ShapeDtypeStruct((bsz, N_CHAN, seqlen), F32),
                   jax.ShapeDtypeStruct((bsz, N_CHAN, seqlen), F32),
                   jax.ShapeDtypeStruct((bsz, seqlen, N_CHAN), F32),
                   jax.ShapeDtypeStruct((bsz, seqlen, N_CHAN), F32)),
        compiler_params=_params("parallel", "parallel"),
        name="gate_prep",
    )(gr)


def _mlstm_kernel(qf_ref, qb_ref, ktf_ref, ktb_ref, vf_ref, vb_ref,
                  arf_ref, arb_ref, brf_ref, brb_ref, ccf_ref, ccb_ref, bcf_ref, bcb_ref,
                  hf_ref, hb_ref, c_scr, m_scr, *, bsz):
    j = pl.program_id(0)

    @pl.when(j == 0)
    def _():
        c_scr[...] = jnp.zeros_like(c_scr)
        m_scr[...] = jnp.zeros_like(m_scr)

    ti = lax.broadcasted_iota(jnp.int32, (CHUNK, CHUNK), 0)
    si = lax.broadcasted_iota(jnp.int32, (CHUNK, CHUNK), 1)
    ones_blk = jnp.ones((CHUNK, A_HEAD_DIM), BF16)
    sub8 = lax.broadcasted_iota(jnp.int32, (N_CHAN, LANES), 0)
    lane8 = lax.broadcasted_iota(jnp.int32, (N_CHAN, LANES), 1)

    dirs = (
        (0, qf_ref, ktf_ref, vf_ref, arf_ref, brf_ref, ccf_ref, bcf_ref, hf_ref, ti >= si, CHUNK - 1),
        (1, qb_ref, ktb_ref, vb_ref, arb_ref, brb_ref, ccb_ref, bcb_ref, hb_ref, ti <= si, 0),
    )
    for d, q_ref, kt_ref, v_ref, ar_ref, br_ref, cc_ref, bc_ref, h_ref, vis, last in dirs:

        def body(b, carry, d=d, q_ref=q_ref, kt_ref=kt_ref, v_ref=v_ref, ar_ref=ar_ref,
                 br_ref=br_ref, cc_ref=cc_ref, bc_ref=bc_ref, h_ref=h_ref, vis=vis, last=last):
            a_row = ar_ref[b]
            b_last = br_ref[b][:, last:last + 1]
            m8 = m_scr[b * 2 + d]
            gl8 = jnp.maximum(m8, jnp.max(a_row, axis=1, keepdims=True))
            w_row = jnp.exp(a_row - gl8)
            decay8 = jnp.exp(m8 - gl8)
            m_scr[b * 2 + d] = b_last + gl8
            m_lane = jnp.sum(jnp.where(sub8 == lane8, m8, 0.0), axis=0, keepdims=True)[:, :N_CHAN]
            g = jnp.maximum(m_lane, cc_ref[b])
            iw = jnp.exp(m_lane - g)
            emt = jnp.exp(-(bc_ref[b] + g))
            for h in range(A_HEADS):
                ch = d * A_HEADS + h
                hs = slice(h * A_HEAD_DIM, (h + 1) * A_HEAD_DIM)
                q = q_ref[b, :, hs]
                kt = kt_ref[b, hs, :]
                v_aug = jnp.concatenate([v_ref[b, :, hs], ones_blk], axis=1)
                p = jnp.where(vis, jnp.exp(a_row[ch:ch + 1, :] - g[:, ch:ch + 1]), 0.0)
                s = (_dot(q, kt) * p).astype(BF16)
                idx = (b * 2 + d) * A_HEADS + h
                c_old = c_scr[idx]
                r = _dot(s, v_aug) + iw[:, ch:ch + 1] * _dot(q, c_old.astype(BF16))
                num = r[:, :A_HEAD_DIM]
                den = r[:, A_HEAD_DIM:]
                h_ref[b, :, hs] = num / jnp.maximum(jnp.abs(den), emt[:, ch:ch + 1])
                ktw = (kt.astype(F32) * w_row[ch:ch + 1, :]).astype(BF16)
                c_scr[idx] = decay8[ch:ch + 1, :1] * c_old + _dot(ktw, v_aug)
            return carry

        lax.fori_loop(0, bsz, body, 0)


def _mlstm(q, kt, va, ar, br, cc, bc):
    bsz, seqlen, _ = q.shape
    nc = seqlen // CHUNK
    fwd3 = lambda j: (0, j, 0)
    bwd3 = lambda j: (0, nc - 1 - j, 0)
    fwd3t = lambda j: (0, 0, j)
    bwd3t = lambda j: (0, 0, nc - 1 - j)
    tok = (bsz, CHUNK, A_WIDTH)
    tok_t = (bsz, A_WIDTH, CHUNK)
    rowb = (bsz, N_CHAN, CHUNK)
    colb = (bsz, CHUNK, N_CHAN)
    in_specs = [
        pl.BlockSpec(tok, fwd3), pl.BlockSpec(tok, bwd3),
        pl.BlockSpec(tok_t, fwd3t), pl.BlockSpec(tok_t, bwd3t),
        pl.BlockSpec(tok, fwd3), pl.BlockSpec(tok, bwd3),
        pl.BlockSpec(rowb, fwd3t), pl.BlockSpec(rowb, bwd3t),
        pl.BlockSpec(rowb, fwd3t), pl.BlockSpec(rowb, bwd3t),
        pl.BlockSpec(colb, fwd3), pl.BlockSpec(colb, bwd3),
        pl.BlockSpec(colb, fwd3), pl.BlockSpec(colb, bwd3),
    ]
    return pl.pallas_call(
        functools.partial(_mlstm_kernel, bsz=bsz),
        grid=(nc,),
        in_specs=in_specs,
        out_specs=(pl.BlockSpec(tok, fwd3), pl.BlockSpec(tok, bwd3)),
        out_shape=(jax.ShapeDtypeStruct((bsz, seqlen, A_WIDTH), F32),
                   jax.ShapeDtypeStruct((bsz, seqlen, A_WIDTH), F32)),
        scratch_shapes=[pltpu.VMEM((bsz * 2 * A_HEADS, A_HEAD_DIM, 2 * A_HEAD_DIM), F32),
                        pltpu.VMEM((bsz * 2, N_CHAN, LANES), F32)],
        compiler_params=_params("arbitrary"),
        name="mlstm",
    )(q, q, kt, kt, va, va, ar, ar, br, br, cc, cc, bc, bc)


def _t5_bucket(rel):
    nb = N_BUCKETS // 2
    max_exact = nb // 2
    ret = jnp.where(rel > 0, nb, 0)
    n = jnp.abs(rel)
    nf = jnp.maximum(n, 1).astype(jnp.float32)
    large = max_exact + (jnp.log(nf / max_exact) / math.log(MAX_DISTANCE / max_exact)
                         * (nb - max_exact)).astype(jnp.int32)
    large = jnp.minimum(large, nb - 1)
    return ret + jnp.where(n < max_exact, n, large)


def _bias_kernel(rb_ref, bucket_ref, bias_ref):
    bucket = bucket_ref[...]
    for hq in range(B_Q_HEADS):
        acc = jnp.zeros(bucket.shape, F32)
        for nb in range(N_BUCKETS):
            acc = jnp.where(bucket == nb, rb_ref[nb, hq], acc)
        bias_ref[hq] = acc


def _bias_table(rel_bias):
    q_off = jnp.arange(BLOCK)
    k_off = jnp.arange(3 * BLOCK) - BLOCK
    bucket = _t5_bucket(k_off[None, :] - q_off[:, None]).astype(jnp.int32)
    return pl.pallas_call(
        _bias_kernel,
        in_specs=[pl.BlockSpec(memory_space=pltpu.SMEM),
                  pl.BlockSpec(bucket.shape, lambda: (0, 0))],
        out_specs=pl.BlockSpec((B_Q_HEADS, BLOCK, 3 * BLOCK), lambda: (0, 0, 0)),
        out_shape=jax.ShapeDtypeStruct((B_Q_HEADS, BLOCK, 3 * BLOCK), F32),
        name="bias_table",
    )(rel_bias.astype(F32), bucket)


def _attn_kernel(sink_ref, q_ref, kp_ref, kc_ref, kn_ref, vp_ref, vc_ref, vn_ref, zb_ref, bias_ref,
                 y_ref):
    j = pl.program_id(1)
    nb = pl.num_programs(1)
    qi = lax.broadcasted_iota(jnp.int32, (BLOCK, 3 * BLOCK), 0)
    kj = lax.broadcasted_iota(jnp.int32, (BLOCK, 3 * BLOCK), 1)
    rel = kj - BLOCK - qi
    key_pos = kj - BLOCK + j * BLOCK
    mask = (jnp.abs(rel) <= WINDOW) & (key_pos >= 0) & (key_pos < nb * BLOCK)
    lo_half = lax.broadcasted_iota(jnp.int32, (3 * BLOCK, LANES), 1) < B_HEAD_DIM
    out_lo = lax.broadcasted_iota(jnp.int32, (BLOCK, LANES), 1) < B_HEAD_DIM
    zero = jnp.zeros((3 * BLOCK, LANES), BF16)
    for h in range(B_KV_HEADS):
        hs = slice(h * LANES, (h + 1) * LANES)
        k2 = jnp.concatenate([kp_ref[0, :, hs], kc_ref[0, :, hs], kn_ref[0, :, hs]], axis=0)
        v2 = jnp.concatenate([vp_ref[0, :, hs], vc_ref[0, :, hs], vn_ref[0, :, hs]], axis=0)
        k_even = jnp.where(lo_half, k2, zero)
        k_odd = jnp.where(lo_half, zero, k2)
        for pair in range(2):
            p_idx = h * 2 + pair
            ps = slice(p_idx * LANES, (p_idx + 1) * LANES)
            qp = q_ref[0, :, ps]
            outs = []
            for par, k_sel in ((0, k_even), (1, k_odd)):
                hq = p_idx * 2 + par
                sc = jnp.where(mask, _dot_nt(qp, k_sel) + bias_ref[hq], NEG_INF)
                sink = sink_ref[hq]
                m = jnp.maximum(jnp.max(sc, axis=-1, keepdims=True), sink)
                p = jnp.exp(sc - m)
                denom = jnp.sum(p, axis=-1, keepdims=True) + jnp.exp(sink - m)
                outs.append(_dot(p.astype(BF16), v2) / denom)
            y = jnp.where(out_lo, outs[0], outs[1])
            y_ref[0, :, ps] = (y * zb_ref[0, :, ps].astype(F32)).astype(BF16)


def _attn(sink, qb, kb2, vb2, zb, bias):
    bsz, seqlen, _ = qb.shape
    nb = seqlen // BLOCK
    cur = lambda b, j: (b, j, 0)
    prev = lambda b, j: (b, jnp.maximum(j - 1, 0), 0)
    nxt = lambda b, j: (b, jnp.minimum(j + 1, nb - 1), 0)
    kvb = (1, BLOCK, 2 * LANES)
    return pl.pallas_call(
        _attn_kernel,
        grid=(bsz, nb),
        in_specs=[pl.BlockSpec(memory_space=pltpu.SMEM),
                  pl.BlockSpec((1, BLOCK, B_WIDTH), cur),
                  pl.BlockSpec(kvb, prev), pl.BlockSpec(kvb, cur), pl.BlockSpec(kvb, nxt),
                  pl.BlockSpec(kvb, prev), pl.BlockSpec(kvb, cur), pl.BlockSpec(kvb, nxt),
                  pl.BlockSpec((1, BLOCK, B_WIDTH), cur),
                  pl.BlockSpec(bias.shape, lambda b, j: (0, 0, 0))],
        out_specs=pl.BlockSpec((1, BLOCK, B_WIDTH), cur),
        out_shape=jax.ShapeDtypeStruct((bsz, seqlen, B_WIDTH), BF16),
        compiler_params=_params("parallel", "parallel"),
        name="attn",
    )(sink, qb, kb2, kb2, kb2, vb2, vb2, vb2, zb, bias)


def _out_proj_kernel(hf_ref, hb_ref, og_ref, zg_ref, yb_ref, x_ref, mw_ref, wo_ref, fw_ref, o_ref,
                     *, final):
    h = og_ref[...].astype(F32) * (hf_ref[...] + hb_ref[...])
    parts = []
    for k in range(A_HEADS):
        hs = slice(k * A_HEAD_DIM, (k + 1) * A_HEAD_DIM)
        hh = h[:, hs]
        hh = hh * lax.rsqrt(jnp.mean(hh * hh, axis=-1, keepdims=True) + EPS)
        parts.append(hh * mw_ref[:, hs])
    ya = (jnp.concatenate(parts, axis=1) * zg_ref[...].astype(F32)).astype(BF16)
    out = x_ref[...] + _dot(ya, wo_ref[:A_WIDTH, :]) + _dot(yb_ref[...], wo_ref[A_WIDTH:, :])
    if final:
        out = out * lax.rsqrt(jnp.mean(out * out, axis=-1, keepdims=True) + EPS) * fw_ref[...]
    o_ref[...] = out


def _out_proj(hf, hb, og, zg, yb, x, mw, wo, fw, *, tm, final):
    n_tok = x.shape[0]

    def rows(width):
        return pl.BlockSpec((tm, width), lambda i: (i, 0))

    def full(arr):
        return pl.BlockSpec(arr.shape, lambda i: (0,) * arr.ndim)

    return pl.pallas_call(
        functools.partial(_out_proj_kernel, final=final),
        grid=(n_tok // tm,),
        in_specs=[rows(A_WIDTH), rows(A_WIDTH), rows(A_WIDTH), rows(A_WIDTH), rows(B_WIDTH),
                  rows(D_MODEL), full(mw), full(wo), full(fw)],
        out_specs=rows(D_MODEL),
        out_shape=jax.ShapeDtypeStruct(x.shape, F32),
        compiler_params=_params("parallel"),
        name="out_proj",
    )(hf, hb, og, zg, yb, x, mw, wo, fw)


def _tile(n, target):
    t = min(n, target)
    assert n % t == 0, (n, t)
    return t


def kernel(x, norm_w, w_in, conv_w, conv_b, gate_b, mhn_w, sink, rel_bias, w_out, final_norm_w):
    bsz, seqlen, d_model = x.shape
    depth = norm_w.shape[0]
    assert d_model == D_MODEL and seqlen % CHUNK == 0
    tm = _tile(seqlen, 512)
    tg = _tile(seqlen, 1024)
    to = _tile(bsz * seqlen, 512)

    c_qk = 2 * A_WIDTH
    c_a = c_qk + 3 * A_WIDTH
    c_g = c_a + N_GATES
    wqk = w_in[:, :, :c_qk].astype(BF16)
    wa = w_in[:, :, c_qk:c_a].astype(BF16)
    wgt = jnp.swapaxes(w_in[:, :, c_a:c_g], 1, 2).astype(BF16)
    wb = w_in[:, :, c_g:].astype(BF16)
    wo = w_out.astype(BF16)
    cw = jnp.pad(conv_w, ((0, 0), (0, SUBLANES - CONV_K), (0, 0)))
    bias = _bias_table(rel_bias)
    fw = final_norm_w.reshape(1, D_MODEL)

    xf = x
    for l in range(depth):
        q, kt, va, og, zg, qb, kb2, vb2, zb, gr = _in_proj(
            xf, norm_w[l].reshape(1, D_MODEL), wqk[l], wa[l], wb[l], wgt[l], cw[l],
            conv_b[l].reshape(1, 2 * A_WIDTH), gate_b[l].reshape(N_GATES, 1), tm=tm)
        ar, br, cc, bc = _gate_prep(gr, tg=tg)
        hf, hb = _mlstm(q, kt, va, ar, br, cc, bc)
        yb = _attn(sink[l], qb, kb2, vb2, zb, bias)
        flat = lambda t: t.reshape(bsz * seqlen, t.shape[-1])
        xf = _out_proj(flat(hf), flat(hb), flat(og), flat(zg), flat(yb), flat(xf),
                       mhn_w[l].reshape(1, A_WIDTH), wo[l], fw, tm=to,
                       final=(l == depth - 1)).reshape(bsz, seqlen, D_MODEL)
    return xf
```

```python
import functools
import math

import jax
import jax.numpy as jnp
import numpy as np
from jax import lax
from jax.experimental import pallas as pl
from jax.experimental.pallas import tpu as pltpu

D_MODEL = 1024
A_WIDTH = 512
A_HEADS = 4
A_HEAD_DIM = 128
CHUNK = 128
CONV_K = 5
B_WIDTH = 512
B_HEAD_DIM = 64
B_Q_HEADS = 8
B_KV_HEADS = 2
WINDOW = 128
BLOCK = 128
N_BUCKETS = 32
MAX_DISTANCE = 128
EPS = 1e-6
NEG_INF = -1e30
LOG2E = math.log2(math.e)
N_GATES = 4 * A_HEADS
N_CHAN = 2 * A_HEADS

LANES = 128
SUBLANES = 8
VMEM_LIMIT_BYTES = 56 * 1024 * 1024

HALO = SUBLANES
BF16 = jnp.bfloat16
F32 = jnp.float32


def _params(*sem):
    return pltpu.CompilerParams(dimension_semantics=sem, vmem_limit_bytes=VMEM_LIMIT_BYTES)


def _dot(a, b):
    return jnp.dot(a, b, preferred_element_type=F32)


def _dot_nt(a, b):
    return lax.dot_general(a, b, (((1,), (1,)), ((), ())), preferred_element_type=F32)


def _log_sigmoid(x):
    return -(jnp.maximum(-x, 0.0) + jnp.log1p(jnp.exp(-jnp.abs(x))))


def _silu(x):
    return x * jax.nn.sigmoid(x)


def _in_proj_kernel(x_ref, xp_ref, xn_ref, nw_ref, wqk_ref, wa_ref, wb_ref, wgt_ref,
                    cw_ref, cb_ref, gb_ref,
                    q_ref, kt_ref, va_ref, og_ref, zg_ref, qb_ref, kb2_ref, vb2_ref, zb_ref, gr_ref,
                    u_scr, *, tm):
    i = pl.program_id(1)
    last = pl.num_programs(1) - 1
    nw = nw_ref[...]

    def norm(xv):
        y = xv * lax.rsqrt(jnp.mean(xv * xv, axis=-1, keepdims=True) + EPS)
        return (y * nw).astype(BF16)

    hn = norm(x_ref[0])
    hp = norm(xp_ref[0])
    hx = norm(xn_ref[0])
    wqk = wqk_ref[...]
    u_scr[HALO:HALO + tm, :] = _dot(hn, wqk)
    u_scr[0:HALO, :] = jnp.where(i > 0, _dot(hp, wqk), 0.0)
    u_scr[HALO + tm:2 * HALO + tm, :] = jnp.where(i < last, _dot(hx, wqk), 0.0)
    acc = u_scr[HALO - 2:HALO - 2 + tm, :] * cw_ref[0:1, :] + cb_ref[...]
    for t in range(1, CONV_K):
        acc = acc + u_scr[HALO - 2 + t:HALO - 2 + t + tm, :] * cw_ref[t:t + 1, :]
    qk = _silu(acc)
    q_ref[0] = qk[:, :A_WIDTH].astype(BF16)
    k = qk[:, A_WIDTH:] * (A_HEAD_DIM ** -0.5)
    kt_ref[0] = k.T.astype(BF16)

    a = _dot(hn, wa_ref[...])
    va_ref[0] = a[:, :A_WIDTH].astype(BF16)
    og_ref[0] = jax.nn.sigmoid(a[:, A_WIDTH:2 * A_WIDTH]).astype(BF16)
    zg_ref[0] = _silu(a[:, 2 * A_WIDTH:]).astype(BF16)

    bq = _dot(hn, wb_ref[:, :B_WIDTH])
    qb_ref[0] = (bq * (B_HEAD_DIM ** -0.5 * LOG2E)).astype(BF16)
    kv = _dot(hn, wb_ref[:, B_WIDTH:B_WIDTH + 2 * LANES])
    half = lax.broadcasted_iota(jnp.int32, (tm, LANES), 1) < B_HEAD_DIM
    for src, dst in ((kv[:, :LANES], kb2_ref), (kv[:, LANES:], vb2_ref)):
        sw = pltpu.roll(src, B_HEAD_DIM, 1)
        dst[0, :, :LANES] = jnp.where(half, src, sw).astype(BF16)
        dst[0, :, LANES:] = jnp.where(half, sw, src).astype(BF16)
    zb_ref[0] = _silu(_dot(hn, wb_ref[:, B_WIDTH + 2 * LANES:])).astype(BF16)

    g = _dot_nt(wgt_ref[...], hn) + gb_ref[...]
    row = lax.broadcasted_iota(jnp.int32, g.shape, 0)
    gr_ref[0] = jnp.where(row < N_CHAN, g, _log_sigmoid(g))


def _in_proj(x, nw, wqk, wa, wb, wgt, cw, cb, gb, *, tm):
    bsz, seqlen, _ = x.shape
    nt = seqlen // tm
    hb = tm // HALO
    nhb = seqlen // HALO

    def full(arr):
        return pl.BlockSpec(arr.shape, lambda b, i: (0,) * arr.ndim)

    def rows(width):
        return pl.BlockSpec((1, tm, width), lambda b, i: (b, i, 0))

    out_shape = (
        jax.ShapeDtypeStruct((bsz, seqlen, A_WIDTH), BF16),
        jax.ShapeDtypeStruct((bsz, A_WIDTH, seqlen), BF16),
        jax.ShapeDtypeStruct((bsz, seqlen, A_WIDTH), BF16),
        jax.ShapeDtypeStruct((bsz, seqlen, A_WIDTH), BF16),
        jax.ShapeDtypeStruct((bsz, seqlen, A_WIDTH), BF16),
        jax.ShapeDtypeStruct((bsz, seqlen, B_WIDTH), BF16),
        jax.ShapeDtypeStruct((bsz, seqlen, 2 * LANES), BF16),
        jax.ShapeDtypeStruct((bsz, seqlen, 2 * LANES), BF16),
        jax.ShapeDtypeStruct((bsz, seqlen, B_WIDTH), BF16),
        jax.ShapeDtypeStruct((bsz, N_GATES, seqlen), F32),
    )
    out_specs = (
        rows(A_WIDTH),
        pl.BlockSpec((1, A_WIDTH, tm), lambda b, i: (b, 0, i)),
        rows(A_WIDTH), rows(A_WIDTH), rows(A_WIDTH), rows(B_WIDTH),
        rows(2 * LANES), rows(2 * LANES), rows(B_WIDTH),
        pl.BlockSpec((1, N_GATES, tm), lambda b, i: (b, 0, i)),
    )
    in_specs = [
        rows(D_MODEL),
        pl.BlockSpec((1, HALO, D_MODEL), lambda b, i: (b, jnp.maximum(i * hb - 1, 0), 0)),
        pl.BlockSpec((1, HALO, D_MODEL), lambda b, i: (b, jnp.minimum((i + 1) * hb, nhb - 1), 0)),
        full(nw), full(wqk), full(wa), full(wb), full(wgt), full(cw), full(cb), full(gb),
    ]
    return pl.pallas_call(
        functools.partial(_in_proj_kernel, tm=tm),
        grid=(bsz, nt),
        in_specs=in_specs,
        out_specs=out_specs,
        out_shape=out_shape,
        scratch_shapes=[pltpu.VMEM((tm + 2 * HALO, 2 * A_WIDTH), F32)],
        compiler_params=_params("parallel", "arbitrary"),
        name="in_proj",
    )(x, x, x, nw, wqk, wa, wb, wgt, cw, cb, gb)


def _gate_prep_kernel(gr_ref, ar_ref, br_ref, cc_ref, bc_ref, *, n_chunks):
    ti = lax.broadcasted_iota(jnp.int32, (CHUNK, CHUNK), 0)
    si = lax.broadcasted_iota(jnp.int32, (CHUNK, CHUNK), 1)
    upper = (ti <= si)
    lower = (ti >= si)
    diag = (ti == si)
    upper_f = upper.astype(F32)
    lower_f = lower.astype(F32)
    chan_row = lax.broadcasted_iota(jnp.int32, (N_CHAN, CHUNK), 0)
    chan_col = lax.broadcasted_iota(jnp.int32, (CHUNK, N_CHAN), 1)
    for c in range(n_chunks):
        sl = slice(c * CHUNK, (c + 1) * CHUNK)
        li = gr_ref[0, :N_CHAN, sl]
        lf = gr_ref[0, N_CHAN:, sl]
        b_pre = jnp.dot(lf, upper_f, preferred_element_type=F32, precision=lax.Precision.HIGHEST)
        b_suf = jnp.dot(lf, lower_f, preferred_element_type=F32, precision=lax.Precision.HIGHEST)
        b = jnp.where(chan_row < A_HEADS, b_pre, b_suf)
        a = li - b
        ar_ref[0, :, sl] = a
        br_ref[0, :, sl] = b
        cc = jnp.zeros((CHUNK, N_CHAN), F32)
        bc = jnp.zeros((CHUNK, N_CHAN), F32)
        for ch in range(N_CHAN):
            vis = lower if ch < A_HEADS else upper
            a_b = jnp.broadcast_to(a[ch:ch + 1, :], (CHUNK, CHUNK))
            b_b = jnp.broadcast_to(b[ch:ch + 1, :], (CHUNK, CHUNK))
            cm = jnp.max(jnp.where(vis, a_b, -jnp.inf), axis=1, keepdims=True)
            bt = jnp.sum(jnp.where(diag, b_b, 0.0), axis=1, keepdims=True)
            cc = jnp.where(chan_col == ch, cm, cc)
            bc = jnp.where(chan_col == ch, bt, bc)
        cc_ref[0, sl, :] = cc
        bc_ref[0, sl, :] = bc


def _gate_prep(gr, *, tg):
    bsz, _, seqlen = gr.shape
    row_spec = pl.BlockSpec((1, N_CHAN, tg), lambda b, i: (b, 0, i))
    col_spec = pl.BlockSpec((1, tg, N_CHAN), lambda b, i: (b, i, 0))
    return pl.pallas_call(
        functools.partial(_gate_prep_kernel, n_chunks=tg // CHUNK),
        grid=(bsz, seqlen // tg),
        in_specs=[pl.BlockSpec((1, N_GATES, tg), lambda b, i: (b, 0, i))],
        out_specs=(row_spec, row_spec, col_spec, col_spec),
        out_shape=(jax.ShapeDtypeStruct((bsz, N_CHAN, seqlen), F32),
                   jax.ShapeDtypeStruct((bsz, N_CHAN, seqlen), F32),
                   jax.ShapeDtypeStruct((bsz, seqlen, N_CHAN), F32),
                   jax.ShapeDtypeStruct((bsz, seqlen, N_CHAN), F32)),
        compiler_params=_params("parallel", "parallel"),
        name="gate_prep",
    )(gr)


def _mlstm_kernel(qf_ref, qb_ref, ktf_ref, ktb_ref, vf_ref, vb_ref,
                  arf_ref, arb_ref, brf_ref, brb_ref, ccf_ref, ccb_ref, bcf_ref, bcb_ref,
                  hf_ref, hb_ref, c_scr, m_scr, *, bsz):
    j = pl.program_id(0)

    @pl.when(j == 0)
    def _():
        c_scr[...] = jnp.zeros_like(c_scr)
        m_scr[...] = jnp.zeros_like(m_scr)

    ti = lax.broadcasted_iota(jnp.int32, (CHUNK, CHUNK), 0)
    si = lax.broadcasted_iota(jnp.int32, (CHUNK, CHUNK), 1)
    ones_blk = jnp.ones((CHUNK, A_HEAD_DIM), BF16)
    sub8 = lax.broadcasted_iota(jnp.int32, (N_CHAN, LANES), 0)
    lane8 = lax.broadcasted_iota(jnp.int32, (N_CHAN, LANES), 1)

    dirs = (
        (0, qf_ref, ktf_ref, vf_ref, arf_ref, brf_ref, ccf_ref, bcf_ref, hf_ref, ti >= si, CHUNK - 1),
        (1, qb_ref, ktb_ref, vb_ref, arb_ref, brb_ref, ccb_ref, bcb_ref, hb_ref, ti <= si, 0),
    )
    def body(b, carry):
        tiles = []
        for d, q_ref, kt_ref, v_ref, ar_ref, br_ref, cc_ref, bc_ref, h_ref, vis, last in dirs:
            for h in range(A_HEADS):
                hs = slice(h * A_HEAD_DIM, (h + 1) * A_HEAD_DIM)
                idx = (b * 2 + d) * A_HEADS + h
                q = q_ref[b, :, hs]
                kt = kt_ref[b, hs, :]
                c_old = c_scr[idx]
                tiles.append((_dot(q, kt), _dot(q, c_old.astype(BF16)), kt, c_old, idx, hs))

        gates = []
        for d, q_ref, kt_ref, v_ref, ar_ref, br_ref, cc_ref, bc_ref, h_ref, vis, last in dirs:
            a_row = ar_ref[b]
            b_last = br_ref[b][:, last:last + 1]
            m8 = m_scr[b * 2 + d]
            gl8 = jnp.maximum(m8, jnp.max(a_row, axis=1, keepdims=True))
            w_row = jnp.exp(a_row - gl8)
            decay8 = jnp.exp(m8 - gl8)
            m_scr[b * 2 + d] = b_last + gl8
            m_lane = jnp.sum(jnp.where(sub8 == lane8, m8, 0.0), axis=0, keepdims=True)[:, :N_CHAN]
            g = jnp.maximum(m_lane, cc_ref[b])
            iw = jnp.exp(m_lane - g)
            emt = jnp.exp(-(bc_ref[b] + g))
            gates.append((a_row, w_row, decay8, g, iw, emt))

        updates = []
        for d, q_ref, kt_ref, v_ref, ar_ref, br_ref, cc_ref, bc_ref, h_ref, vis, last in dirs:
            a_row, w_row, decay8, g, iw, emt = gates[d]
            for h in range(A_HEADS):
                ch = d * A_HEADS + h
                qk, qc, kt, c_old, idx, hs = tiles[ch]
                v_aug = jnp.concatenate([v_ref[b, :, hs], ones_blk], axis=1)
                p = jnp.where(vis, jnp.exp(a_row[ch:ch + 1, :] - g[:, ch:ch + 1]), 0.0)
                s = (qk * p).astype(BF16)
                r = _dot(s, v_aug) + iw[:, ch:ch + 1] * qc
                num = r[:, :A_HEAD_DIM]
                den = r[:, A_HEAD_DIM:]
                h_ref[b, :, hs] = num / jnp.maximum(jnp.abs(den), emt[:, ch:ch + 1])
                ktw = (kt.astype(F32) * w_row[ch:ch + 1, :]).astype(BF16)
                updates.append((idx, decay8[ch:ch + 1, :1] * c_old, ktw, v_aug))

        for idx, c_dec, ktw, v_aug in updates:
            c_scr[idx] = c_dec + _dot(ktw, v_aug)
        return carry

    lax.fori_loop(0, bsz, body, 0)


def _mlstm(q, kt, va, ar, br, cc, bc):
    bsz, seqlen, _ = q.shape
    nc = seqlen // CHUNK
    fwd3 = lambda j: (0, j, 0)
    bwd3 = lambda j: (0, nc - 1 - j, 0)
    fwd3t = lambda j: (0, 0, j)
    bwd3t = lambda j: (0, 0, nc - 1 - j)
    tok = (bsz, CHUNK, A_WIDTH)
    tok_t = (bsz, A_WIDTH, CHUNK)
    rowb = (bsz, N_CHAN, CHUNK)
    colb = (bsz, CHUNK, N_CHAN)
    in_specs = [
        pl.BlockSpec(tok, fwd3), pl.BlockSpec(tok, bwd3),
        pl.BlockSpec(tok_t, fwd3t), pl.BlockSpec(tok_t, bwd3t),
        pl.BlockSpec(tok, fwd3), pl.BlockSpec(tok, bwd3),
        pl.BlockSpec(rowb, fwd3t), pl.BlockSpec(rowb, bwd3t),
        pl.BlockSpec(rowb, fwd3t), pl.BlockSpec(rowb, bwd3t),
        pl.BlockSpec(colb, fwd3), pl.BlockSpec(colb, bwd3),
        pl.BlockSpec(colb, fwd3), pl.BlockSpec(colb, bwd3),
    ]
    return pl.pallas_call(
        functools.partial(_mlstm_kernel, bsz=bsz),
        grid=(nc,),
        in_specs=in_specs,
        out_specs=(pl.BlockSpec(tok, fwd3), pl.BlockSpec(tok, bwd3)),
        out_shape=(jax.ShapeDtypeStruct((bsz, seqlen, A_WIDTH), F32),
                   jax.ShapeDtypeStruct((bsz, seqlen, A_WIDTH), F32)),
        scratch_shapes=[pltpu.VMEM((bsz * 2 * A_HEADS, A_HEAD_DIM, 2 * A_HEAD_DIM), F32),
                        pltpu.VMEM((bsz * 2, N_CHAN, LANES), F32)],
        compiler_params=_params("arbitrary"),
        name="mlstm",
    )(q, q, kt, kt, va, va, ar, ar, br, br, cc, cc, bc, bc)


def _t5_bucket(rel):
    nb = N_BUCKETS // 2
    max_exact = nb // 2
    ret = jnp.where(rel > 0, nb, 0)
    n = jnp.abs(rel)
    nf = jnp.maximum(n, 1).astype(jnp.float32)
    large = max_exact + (jnp.log(nf / max_exact) / math.log(MAX_DISTANCE / max_exact)
                         * (nb - max_exact)).astype(jnp.int32)
    large = jnp.minimum(large, nb - 1)
    return ret + jnp.where(n < max_exact, n, large)


def _bias_kernel(rb_ref, bucket_ref, bias_ref):
    bucket = bucket_ref[...]
    qi = lax.broadcasted_iota(jnp.int32, bucket.shape, 0)
    kj = lax.broadcasted_iota(jnp.int32, bucket.shape, 1)
    band = jnp.abs(kj - BLOCK - qi) <= WINDOW
    masks = (band & (kj >= BLOCK), band, band & (kj < 2 * BLOCK))
    for hq in range(B_Q_HEADS):
        acc = jnp.zeros(bucket.shape, F32)
        for nb in range(N_BUCKETS):
            acc = jnp.where(bucket == nb, rb_ref[nb, hq], acc)
        acc = acc * LOG2E
        for v, mask in enumerate(masks):
            bias_ref[v, hq] = jnp.where(mask, acc, NEG_INF)


def _bias_table(rel_bias):
    q_off = jnp.arange(BLOCK)
    k_off = jnp.arange(3 * BLOCK) - BLOCK
    bucket = _t5_bucket(k_off[None, :] - q_off[:, None]).astype(jnp.int32)
    shape = (3, B_Q_HEADS, BLOCK, 3 * BLOCK)
    return pl.pallas_call(
        _bias_kernel,
        in_specs=[pl.BlockSpec(memory_space=pltpu.SMEM),
                  pl.BlockSpec(bucket.shape, lambda: (0, 0))],
        out_specs=pl.BlockSpec(shape, lambda: (0, 0, 0, 0)),
        out_shape=jax.ShapeDtypeStruct(shape, F32),
        name="bias_table",
    )(rel_bias.astype(F32), bucket)


def _attn_kernel(sink_ref, q_ref, kp_ref, kc_ref, kn_ref, vp_ref, vc_ref, vn_ref, zb_ref, bias_ref,
                 y_ref, s_scr, *, n_qb):
    j = pl.program_id(1)
    last = pl.num_programs(1) - 1
    n_keys = (n_qb + 2) * BLOCK
    lo_k = lax.broadcasted_iota(jnp.int32, (n_keys, LANES), 1) < B_HEAD_DIM
    lo_q = lax.broadcasted_iota(jnp.int32, (BLOCK, LANES), 1) < B_HEAD_DIM
    zero = jnp.zeros((n_keys, LANES), BF16)
    k_sel, v_sel = [], []
    for h in range(B_KV_HEADS):
        hs = slice(h * LANES, (h + 1) * LANES)
        k2 = jnp.concatenate([kp_ref[0, :, hs], kc_ref[0, :, hs], kn_ref[0, :, hs]], axis=0)
        v2 = jnp.concatenate([vp_ref[0, :, hs], vc_ref[0, :, hs], vn_ref[0, :, hs]], axis=0)
        k_sel.append((jnp.where(lo_k, k2, zero), jnp.where(lo_k, zero, k2)))
        v_sel.append((jnp.where(lo_k, v2, zero), jnp.where(lo_k, zero, v2)))

    for i in range(n_qb):
        variant = jnp.int32(1)
        if i == 0:
            variant = jnp.where(j == 0, 0, variant)
        if i == n_qb - 1:
            variant = jnp.where(j == last, 2, variant)
        rows = slice(i * BLOCK, (i + 1) * BLOCK)
        win = slice(i * BLOCK, (i + 3) * BLOCK)
        for p_idx in range(B_Q_HEADS // 2):
            qp = q_ref[0, rows, p_idx * LANES:(p_idx + 1) * LANES]
            for par in range(2):
                hq = p_idx * 2 + par
                k_win = k_sel[p_idx // 2][par][win]
                s_scr[i * B_Q_HEADS + hq] = _dot_nt(qp, k_win) + bias_ref[variant, hq]

    for i in range(n_qb):
        rows = slice(i * BLOCK, (i + 1) * BLOCK)
        win = slice(i * BLOCK, (i + 3) * BLOCK)
        for p_idx in range(B_Q_HEADS // 2):
            ps = slice(p_idx * LANES, (p_idx + 1) * LANES)
            probs, dens = [], []
            for par in range(2):
                hq = p_idx * 2 + par
                sink = sink_ref[hq] * LOG2E
                sc = s_scr[i * B_Q_HEADS + hq]
                m = jnp.maximum(jnp.max(sc, axis=-1, keepdims=True), sink)
                p = jnp.exp2(sc - m)
                dens.append(jnp.sum(p, axis=-1, keepdims=True) + jnp.exp2(sink - m))
                probs.append(p.astype(BF16))
            v_even, v_odd = v_sel[p_idx // 2]
            v_bd = jnp.concatenate([v_even[win], v_odd[win]], axis=0)
            out = _dot(jnp.concatenate(probs, axis=1), v_bd)
            y = out / jnp.where(lo_q, dens[0], dens[1])
            y_ref[0, rows, ps] = (y * zb_ref[0, rows, ps].astype(F32)).astype(BF16)


def _attn(sink, qb, kb2, vb2, zb, bias, *, n_qb):
    bsz, seqlen, _ = qb.shape
    nb = seqlen // BLOCK
    assert nb >= 2 and nb % n_qb == 0
    cur = lambda b, j: (b, j, 0)
    prev = lambda b, j: (b, jnp.maximum(j * n_qb - 1, 0), 0)
    nxt = lambda b, j: (b, jnp.minimum((j + 1) * n_qb, nb - 1), 0)
    halo = (1, BLOCK, 2 * LANES)
    kv_cur = (1, n_qb * BLOCK, 2 * LANES)
    tok = (1, n_qb * BLOCK, B_WIDTH)
    return pl.pallas_call(
        functools.partial(_attn_kernel, n_qb=n_qb),
        grid=(bsz, nb // n_qb),
        in_specs=[pl.BlockSpec(memory_space=pltpu.SMEM),
                  pl.BlockSpec(tok, cur),
                  pl.BlockSpec(halo, prev), pl.BlockSpec(kv_cur, cur), pl.BlockSpec(halo, nxt),
                  pl.BlockSpec(halo, prev), pl.BlockSpec(kv_cur, cur), pl.BlockSpec(halo, nxt),
                  pl.BlockSpec(tok, cur),
                  pl.BlockSpec(bias.shape, lambda b, j: (0, 0, 0, 0))],
        out_specs=pl.BlockSpec(tok, cur),
        out_shape=jax.ShapeDtypeStruct((bsz, seqlen, B_WIDTH), BF16),
        scratch_shapes=[pltpu.VMEM((n_qb * B_Q_HEADS, BLOCK, 3 * BLOCK), F32)],
        compiler_params=_params("parallel", "parallel"),
        name="attn",
    )(sink, qb, kb2, kb2, kb2, vb2, vb2, vb2, zb, bias)


def _out_proj_kernel(hf_ref, hb_ref, og_ref, zg_ref, yb_ref, x_ref, mw_ref, wo_ref, fw_ref, o_ref,
                     *, final):
    h = og_ref[...].astype(F32) * (hf_ref[...] + hb_ref[...])
    parts = []
    for k in range(A_HEADS):
        hs = slice(k * A_HEAD_DIM, (k + 1) * A_HEAD_DIM)
        hh = h[:, hs]
        hh = hh * lax.rsqrt(jnp.mean(hh * hh, axis=-1, keepdims=True) + EPS)
        parts.append(hh * mw_ref[:, hs])
    ya = (jnp.concatenate(parts, axis=1) * zg_ref[...].astype(F32)).astype(BF16)
    out = x_ref[...] + _dot(ya, wo_ref[:A_WIDTH, :]) + _dot(yb_ref[...], wo_ref[A_WIDTH:, :])
    if final:
        out = out * lax.rsqrt(jnp.mean(out * out, axis=-1, keepdims=True) + EPS) * fw_ref[...]
    o_ref[...] = out


def _out_proj(hf, hb, og, zg, yb, x, mw, wo, fw, *, tm, final):
    n_tok = x.shape[0]

    def rows(width):
        return pl.BlockSpec((tm, width), lambda i: (i, 0))

    def full(arr):
        return pl.BlockSpec(arr.shape, lambda i: (0,) * arr.ndim)

    return pl.pallas_call(
        functools.partial(_out_proj_kernel, final=final),
        grid=(n_tok // tm,),
        in_specs=[rows(A_WIDTH), rows(A_WIDTH), rows(A_WIDTH), rows(A_WIDTH), rows(B_WIDTH),
                  rows(D_MODEL), full(mw), full(wo), full(fw)],
        out_specs=rows(D_MODEL),
        out_shape=jax.ShapeDtypeStruct(x.shape, F32),
        compiler_params=_params("parallel"),
        name="out_proj",
    )(hf, hb, og, zg, yb, x, mw, wo, fw)


def _tile(n, target):
    t = min(n, target)
    assert n % t == 0, (n, t)
    return t


def kernel(x, norm_w, w_in, conv_w, conv_b, gate_b, mhn_w, sink, rel_bias, w_out, final_norm_w):
    bsz, seqlen, d_model = x.shape
    depth = norm_w.shape[0]
    assert d_model == D_MODEL and seqlen % CHUNK == 0
    tm = _tile(seqlen, 512)
    tg = _tile(seqlen, 1024)
    to = _tile(bsz * seqlen, 512)
    n_qb = _tile(seqlen // BLOCK, 4)

    c_qk = 2 * A_WIDTH
    c_a = c_qk + 3 * A_WIDTH
    c_g = c_a + N_GATES
    wqk = w_in[:, :, :c_qk].astype(BF16)
    wa = w_in[:, :, c_qk:c_a].astype(BF16)
    wgt = jnp.swapaxes(w_in[:, :, c_a:c_g], 1, 2).astype(BF16)
    wb = w_in[:, :, c_g:].astype(BF16)
    wo = w_out.astype(BF16)
    cw = jnp.pad(conv_w, ((0, 0), (0, SUBLANES - CONV_K), (0, 0)))
    bias = _bias_table(rel_bias)
    fw = final_norm_w.reshape(1, D_MODEL)

    xf = x
    for l in range(depth):
        q, kt, va, og, zg, qb, kb2, vb2, zb, gr = _in_proj(
            xf, norm_w[l].reshape(1, D_MODEL), wqk[l], wa[l], wb[l], wgt[l], cw[l],
            conv_b[l].reshape(1, 2 * A_WIDTH), gate_b[l].reshape(N_GATES, 1), tm=tm)
        ar, br, cc, bc = _gate_prep(gr, tg=tg)
        hf, hb = _mlstm(q, kt, va, ar, br, cc, bc)
        yb = _attn(sink[l], qb, kb2, vb2, zb, bias, n_qb=n_qb)
        flat = lambda t: t.reshape(bsz * seqlen, t.shape[-1])
        xf = _out_proj(flat(hf), flat(hb), flat(og), flat(zg), flat(yb), flat(xf),
                       mhn_w[l].reshape(1, A_WIDTH), wo[l], fw, tm=to,
                       final=(l == depth - 1)).reshape(bsz, seqlen, D_MODEL)
    return xf
```

```python
import functools
import math

import jax
import jax.numpy as jnp
import numpy as np
from jax import lax
from jax.experimental import pallas as pl
from jax.experimental.pallas import tpu as pltpu

D_MODEL = 1024
A_WIDTH = 512
A_HEADS = 4
A_HEAD_DIM = 128
CHUNK = 128
CONV_K = 5
B_WIDTH = 512
B_HEAD_DIM = 64
B_Q_HEADS = 8
B_KV_HEADS = 2
WINDOW = 128
BLOCK = 128
N_BUCKETS = 32
MAX_DISTANCE = 128
EPS = 1e-6
NEG_INF = -1e30
LOG2E = math.log2(math.e)
N_GATES = 4 * A_HEADS
N_CHAN = 2 * A_HEADS

LANES = 128
SUBLANES = 8
VMEM_LIMIT_BYTES = 56 * 1024 * 1024

HALO = 2 * SUBLANES
BF16 = jnp.bfloat16
F32 = jnp.float32


def _params(*sem):
    return pltpu.CompilerParams(dimension_semantics=sem, vmem_limit_bytes=VMEM_LIMIT_BYTES)


def _dot(a, b):
    return jnp.dot(a, b, preferred_element_type=F32)


def _dot_nt(a, b):
    return lax.dot_general(a, b, (((1,), (1,)), ((), ())), preferred_element_type=F32)


def _log_sigmoid(x):
    return -(jnp.maximum(-x, 0.0) + jnp.log1p(jnp.exp(-jnp.abs(x))))


def _silu(x):
    return x * jax.nn.sigmoid(x)


def _in_proj_kernel(x_ref, xp_ref, xn_ref, nw_ref, wqk_ref, wa_ref, wb_ref, wgt_ref,
                    cw_ref, cb_ref, gb_ref,
                    q_ref, kt_ref, va_ref, og_ref, zg_ref, qb_ref, kb2_ref, vb2_ref, zb_ref, gr_ref,
                    u_scr, *, tm):
    i = pl.program_id(1)
    last = pl.num_programs(1) - 1
    nw = nw_ref[...]

    def norm(xv):
        y = xv * lax.rsqrt(jnp.mean(xv * xv, axis=-1, keepdims=True) + EPS)
        return (y * nw).astype(BF16)

    hn = norm(x_ref[0])
    hp = norm(xp_ref[0])
    hx = norm(xn_ref[0])
    u_all = _dot(jnp.concatenate([hp, hn, hx], axis=0), wqk_ref[...])
    rr = lax.broadcasted_iota(jnp.int32, (tm + 2 * HALO, 1), 0)
    outside = ((rr < HALO) & (i == 0)) | ((rr >= HALO + tm) & (i == last))
    u_all = jnp.where(outside, 0.0, u_all)
    parts = []
    pad = CONV_K // 2
    for c in range(2 * A_WIDTH // LANES):
        cs = slice(c * LANES, (c + 1) * LANES)
        u_scr[c] = u_all[:, cs]
        acc = cb_ref[:, cs]
        for tap in range(CONV_K):
            lo = HALO - pad + tap
            acc = acc + u_scr[c, lo:lo + tm, :] * cw_ref[tap:tap + 1, cs]
        parts.append(acc)
    qk = _silu(jnp.concatenate(parts, axis=1))
    q_ref[0] = qk[:, :A_WIDTH].astype(BF16)
    k = qk[:, A_WIDTH:] * (A_HEAD_DIM ** -0.5)
    kt_ref[0] = k.T.astype(BF16)

    a = _dot(hn, wa_ref[...])
    va_ref[0] = a[:, :A_WIDTH].astype(BF16)
    og_ref[0] = jax.nn.sigmoid(a[:, A_WIDTH:2 * A_WIDTH]).astype(BF16)
    zg_ref[0] = _silu(a[:, 2 * A_WIDTH:]).astype(BF16)

    bq = _dot(hn, wb_ref[:, :B_WIDTH])
    qb_ref[0] = (bq * (B_HEAD_DIM ** -0.5 * LOG2E)).astype(BF16)
    kv = _dot(hn, wb_ref[:, B_WIDTH:B_WIDTH + 2 * LANES])
    half = lax.broadcasted_iota(jnp.int32, (tm, LANES), 1) < B_HEAD_DIM
    for src, dst in ((kv[:, :LANES], kb2_ref), (kv[:, LANES:], vb2_ref)):
        sw = pltpu.roll(src, B_HEAD_DIM, 1)
        dst[0, :, :LANES] = jnp.where(half, src, sw).astype(BF16)
        dst[0, :, LANES:] = jnp.where(half, sw, src).astype(BF16)
    zb_ref[0] = _silu(_dot(hn, wb_ref[:, B_WIDTH + 2 * LANES:])).astype(BF16)

    g = _dot_nt(wgt_ref[...], hn) + gb_ref[...]
    row = lax.broadcasted_iota(jnp.int32, g.shape, 0)
    gr_ref[0] = jnp.where(row < N_CHAN, g, _log_sigmoid(g))


def _in_proj(x, nw, wqk, wa, wb, wgt, cw, cb, gb, *, tm):
    bsz, seqlen, _ = x.shape
    nt = seqlen // tm
    hb = tm // HALO
    nhb = seqlen // HALO

    def full(arr):
        return pl.BlockSpec(arr.shape, lambda b, i: (0,) * arr.ndim)

    def rows(width):
        return pl.BlockSpec((1, tm, width), lambda b, i: (b, i, 0))

    out_shape = (
        jax.ShapeDtypeStruct((bsz, seqlen, A_WIDTH), BF16),
        jax.ShapeDtypeStruct((bsz, A_WIDTH, seqlen), BF16),
        jax.ShapeDtypeStruct((bsz, seqlen, A_WIDTH), BF16),
        jax.ShapeDtypeStruct((bsz, seqlen, A_WIDTH), BF16),
        jax.ShapeDtypeStruct((bsz, seqlen, A_WIDTH), BF16),
        jax.ShapeDtypeStruct((bsz, seqlen, B_WIDTH), BF16),
        jax.ShapeDtypeStruct((bsz, seqlen, 2 * LANES), BF16),
        jax.ShapeDtypeStruct((bsz, seqlen, 2 * LANES), BF16),
        jax.ShapeDtypeStruct((bsz, seqlen, B_WIDTH), BF16),
        jax.ShapeDtypeStruct((bsz, N_GATES, seqlen), F32),
    )
    out_specs = (
        rows(A_WIDTH),
        pl.BlockSpec((1, A_WIDTH, tm), lambda b, i: (b, 0, i)),
        rows(A_WIDTH), rows(A_WIDTH), rows(A_WIDTH), rows(B_WIDTH),
        rows(2 * LANES), rows(2 * LANES), rows(B_WIDTH),
        pl.BlockSpec((1, N_GATES, tm), lambda b, i: (b, 0, i)),
    )
    in_specs = [
        rows(D_MODEL),
        pl.BlockSpec((1, HALO, D_MODEL), lambda b, i: (b, jnp.maximum(i * hb - 1, 0), 0)),
        pl.BlockSpec((1, HALO, D_MODEL), lambda b, i: (b, jnp.minimum((i + 1) * hb, nhb - 1), 0)),
        full(nw), full(wqk), full(wa), full(wb), full(wgt), full(cw), full(cb), full(gb),
    ]
    return pl.pallas_call(
        functools.partial(_in_proj_kernel, tm=tm),
        grid=(bsz, nt),
        in_specs=in_specs,
        out_specs=out_specs,
        out_shape=out_shape,
        scratch_shapes=[pltpu.VMEM((2 * A_WIDTH // LANES, tm + 2 * HALO, LANES), F32)],
        compiler_params=_params("parallel", "arbitrary"),
        name="in_proj",
    )(x, x, x, nw, wqk, wa, wb, wgt, cw, cb, gb)


def _gate_prep_kernel(gr_ref, ar_ref, br_ref, cc_ref, bc_ref, *, n_chunks):
    ti = lax.broadcasted_iota(jnp.int32, (CHUNK, CHUNK), 0)
    si = lax.broadcasted_iota(jnp.int32, (CHUNK, CHUNK), 1)
    upper = (ti <= si)
    lower = (ti >= si)
    diag = (ti == si)
    upper_f = upper.astype(F32)
    lower_f = lower.astype(F32)
    chan_row = lax.broadcasted_iota(jnp.int32, (N_CHAN, CHUNK), 0)
    chan_col = lax.broadcasted_iota(jnp.int32, (CHUNK, N_CHAN), 1)
    for c in range(n_chunks):
        sl = slice(c * CHUNK, (c + 1) * CHUNK)
        li = gr_ref[0, :N_CHAN, sl]
        lf = gr_ref[0, N_CHAN:, sl]
        b_pre = jnp.dot(lf, upper_f, preferred_element_type=F32, precision=lax.Precision.HIGHEST)
        b_suf = jnp.dot(lf, lower_f, preferred_element_type=F32, precision=lax.Precision.HIGHEST)
        b = jnp.where(chan_row < A_HEADS, b_pre, b_suf)
        a = li - b
        ar_ref[0, :, sl] = a
        br_ref[0, :, sl] = b
        cc = jnp.zeros((CHUNK, N_CHAN), F32)
        bc = jnp.zeros((CHUNK, N_CHAN), F32)
        for ch in range(N_CHAN):
            vis = lower if ch < A_HEADS else upper
            a_b = jnp.broadcast_to(a[ch:ch + 1, :], (CHUNK, CHUNK))
            b_b = jnp.broadcast_to(b[ch:ch + 1, :], (CHUNK, CHUNK))
            cm = jnp.max(jnp.where(vis, a_b, -jnp.inf), axis=1, keepdims=True)
            bt = jnp.sum(jnp.where(diag, b_b, 0.0), axis=1, keepdims=True)
            cc = jnp.where(chan_col == ch, cm, cc)
            bc = jnp.where(chan_col == ch, bt, bc)
        cc_ref[0, sl, :] = cc
        bc_ref[0, sl, :] = bc


def _gate_prep(gr, *, tg):
    bsz, _, seqlen = gr.shape
    row_spec = pl.BlockSpec((1, N_CHAN, tg), lambda b, i: (b, 0, i))
    col_spec = pl.BlockSpec((1, tg, N_CHAN), lambda b, i: (b, i, 0))
    return pl.pallas_call(
        functools.partial(_gate_prep_kernel, n_chunks=tg // CHUNK),
        grid=(bsz, seqlen // tg),
        in_specs=[pl.BlockSpec((1, N_GATES, tg), lambda b, i: (b, 0, i))],
        out_specs=(row_spec, row_spec, col_spec, col_spec),
        out_shape=(jax.ShapeDtypeStruct((bsz, N_CHAN, seqlen), F32),
                   jax.ShapeDtypeStruct((bsz, N_CHAN, seqlen), F32),
                   jax.ShapeDtypeStruct((bsz, seqlen, N_CHAN), F32),
                   jax.ShapeDtypeStruct((bsz, seqlen, N_CHAN), F32)),
        compiler_params=_params("parallel", "parallel"),
        name="gate_prep",
    )(gr)


def _mlstm_kernel(qf_ref, qb_ref, ktf_ref, ktb_ref, vf_ref, vb_ref,
                  arf_ref, arb_ref, brf_ref, brb_ref, ccf_ref, ccb_ref, bcf_ref, bcb_ref,
                  hf_ref, hb_ref, c_scr, m_scr, *, bsz):
    j = pl.program_id(0)

    @pl.when(j == 0)
    def _():
        c_scr[...] = jnp.zeros_like(c_scr)
        m_scr[...] = jnp.zeros_like(m_scr)

    ti = lax.broadcasted_iota(jnp.int32, (CHUNK, CHUNK), 0)
    si = lax.broadcasted_iota(jnp.int32, (CHUNK, CHUNK), 1)
    ones_blk = jnp.ones((CHUNK, A_HEAD_DIM), BF16)
    sub8 = lax.broadcasted_iota(jnp.int32, (N_CHAN, LANES), 0)
    lane8 = lax.broadcasted_iota(jnp.int32, (N_CHAN, LANES), 1)

    dirs = (
        (0, qf_ref, ktf_ref, vf_ref, arf_ref, brf_ref, ccf_ref, bcf_ref, hf_ref, ti >= si, CHUNK - 1),
        (1, qb_ref, ktb_ref, vb_ref, arb_ref, brb_ref, ccb_ref, bcb_ref, hb_ref, ti <= si, 0),
    )
    def body(b, carry):
        tiles = []
        for d, q_ref, kt_ref, v_ref, ar_ref, br_ref, cc_ref, bc_ref, h_ref, vis, last in dirs:
            for h in range(A_HEADS):
                hs = slice(h * A_HEAD_DIM, (h + 1) * A_HEAD_DIM)
                idx = (b * 2 + d) * A_HEADS + h
                q = q_ref[b, :, hs]
                kt = kt_ref[b, hs, :]
                c_old = c_scr[idx]
                tiles.append((_dot(q, kt), _dot(q, c_old.astype(BF16)), kt, c_old, idx, hs))

        gates = []
        for d, q_ref, kt_ref, v_ref, ar_ref, br_ref, cc_ref, bc_ref, h_ref, vis, last in dirs:
            a_row = ar_ref[b]
            b_last = br_ref[b][:, last:last + 1]
            m8 = m_scr[b * 2 + d]
            gl8 = jnp.maximum(m8, jnp.max(a_row, axis=1, keepdims=True))
            w_row = jnp.exp(a_row - gl8)
            decay8 = jnp.exp(m8 - gl8)
            m_scr[b * 2 + d] = b_last + gl8
            m_lane = jnp.sum(jnp.where(sub8 == lane8, m8, 0.0), axis=0, keepdims=True)[:, :N_CHAN]
            g = jnp.maximum(m_lane, cc_ref[b])
            iw = jnp.exp(m_lane - g)
            emt = jnp.exp(-(bc_ref[b] + g))
            gates.append((a_row, w_row, decay8, g, iw, emt))

        updates = []
        for d, q_ref, kt_ref, v_ref, ar_ref, br_ref, cc_ref, bc_ref, h_ref, vis, last in dirs:
            a_row, w_row, decay8, g, iw, emt = gates[d]
            for h in range(A_HEADS):
                ch = d * A_HEADS + h
                qk, qc, kt, c_old, idx, hs = tiles[ch]
                v_aug = jnp.concatenate([v_ref[b, :, hs], ones_blk], axis=1)
                p = jnp.where(vis, jnp.exp(a_row[ch:ch + 1, :] - g[:, ch:ch + 1]), 0.0)
                s = (qk * p).astype(BF16)
                r = _dot(s, v_aug) + iw[:, ch:ch + 1] * qc
                num = r[:, :A_HEAD_DIM]
                den = r[:, A_HEAD_DIM:]
                h_ref[b, :, hs] = (num / jnp.maximum(jnp.abs(den), emt[:, ch:ch + 1])).astype(BF16)
                ktw = (kt.astype(F32) * w_row[ch:ch + 1, :]).astype(BF16)
                updates.append((idx, decay8[ch:ch + 1, :1] * c_old, ktw, v_aug))

        for idx, c_dec, ktw, v_aug in updates:
            c_scr[idx] = c_dec + _dot(ktw, v_aug)
        return carry

    lax.fori_loop(0, bsz, body, 0)


def _mlstm(q, kt, va, ar, br, cc, bc):
    bsz, seqlen, _ = q.shape
    nc = seqlen // CHUNK
    fwd3 = lambda j: (0, j, 0)
    bwd3 = lambda j: (0, nc - 1 - j, 0)
    fwd3t = lambda j: (0, 0, j)
    bwd3t = lambda j: (0, 0, nc - 1 - j)
    tok = (bsz, CHUNK, A_WIDTH)
    tok_t = (bsz, A_WIDTH, CHUNK)
    rowb = (bsz, N_CHAN, CHUNK)
    colb = (bsz, CHUNK, N_CHAN)
    in_specs = [
        pl.BlockSpec(tok, fwd3), pl.BlockSpec(tok, bwd3),
        pl.BlockSpec(tok_t, fwd3t), pl.BlockSpec(tok_t, bwd3t),
        pl.BlockSpec(tok, fwd3), pl.BlockSpec(tok, bwd3),
        pl.BlockSpec(rowb, fwd3t), pl.BlockSpec(rowb, bwd3t),
        pl.BlockSpec(rowb, fwd3t), pl.BlockSpec(rowb, bwd3t),
        pl.BlockSpec(colb, fwd3), pl.BlockSpec(colb, bwd3),
        pl.BlockSpec(colb, fwd3), pl.BlockSpec(colb, bwd3),
    ]
    return pl.pallas_call(
        functools.partial(_mlstm_kernel, bsz=bsz),
        grid=(nc,),
        in_specs=in_specs,
        out_specs=(pl.BlockSpec(tok, fwd3), pl.BlockSpec(tok, bwd3)),
        out_shape=(jax.ShapeDtypeStruct((bsz, seqlen, A_WIDTH), BF16),
                   jax.ShapeDtypeStruct((bsz, seqlen, A_WIDTH), BF16)),
        scratch_shapes=[pltpu.VMEM((bsz * 2 * A_HEADS, A_HEAD_DIM, 2 * A_HEAD_DIM), F32),
                        pltpu.VMEM((bsz * 2, N_CHAN, LANES), F32)],
        compiler_params=_params("arbitrary"),
        name="mlstm",
    )(q, q, kt, kt, va, va, ar, ar, br, br, cc, cc, bc, bc)


def _t5_bucket(rel):
    nb = N_BUCKETS // 2
    max_exact = nb // 2
    ret = jnp.where(rel > 0, nb, 0)
    n = jnp.abs(rel)
    nf = jnp.maximum(n, 1).astype(jnp.float32)
    large = max_exact + (jnp.log(nf / max_exact) / math.log(MAX_DISTANCE / max_exact)
                         * (nb - max_exact)).astype(jnp.int32)
    large = jnp.minimum(large, nb - 1)
    return ret + jnp.where(n < max_exact, n, large)


def _bias_kernel(rb_ref, bucket_ref, bias_ref):
    bucket = bucket_ref[...]
    qi = lax.broadcasted_iota(jnp.int32, bucket.shape, 0)
    kj = lax.broadcasted_iota(jnp.int32, bucket.shape, 1)
    band = jnp.abs(kj - BLOCK - qi) <= WINDOW
    masks = (band & (kj >= BLOCK), band, band & (kj < 2 * BLOCK))
    for hq in range(B_Q_HEADS):
        acc = jnp.zeros(bucket.shape, F32)
        for nb in range(N_BUCKETS):
            acc = jnp.where(bucket == nb, rb_ref[nb, hq], acc)
        acc = acc * LOG2E
        for v, mask in enumerate(masks):
            bias_ref[v, hq] = jnp.where(mask, acc, NEG_INF)


def _bias_table(rel_bias):
    q_off = jnp.arange(BLOCK)
    k_off = jnp.arange(3 * BLOCK) - BLOCK
    bucket = _t5_bucket(k_off[None, :] - q_off[:, None]).astype(jnp.int32)
    shape = (3, B_Q_HEADS, BLOCK, 3 * BLOCK)
    return pl.pallas_call(
        _bias_kernel,
        in_specs=[pl.BlockSpec(memory_space=pltpu.SMEM),
                  pl.BlockSpec(bucket.shape, lambda: (0, 0))],
        out_specs=pl.BlockSpec(shape, lambda: (0, 0, 0, 0)),
        out_shape=jax.ShapeDtypeStruct(shape, F32),
        name="bias_table",
    )(rel_bias.astype(F32), bucket)


def _attn_out_kernel(sink_ref, q_ref, kp_ref, kc_ref, kn_ref, vp_ref, vc_ref, vn_ref, zb_ref,
                     bias_ref, hf_ref, hb_ref, og_ref, zg_ref, x_ref, mw_ref, wo_ref, fw_ref,
                     o_ref, s_scr, yb_scr, *, n_qb, final):
    j = pl.program_id(1)
    last = pl.num_programs(1) - 1
    n_keys = (n_qb + 2) * BLOCK
    lo_k = lax.broadcasted_iota(jnp.int32, (n_keys, LANES), 1) < B_HEAD_DIM
    lo_q = lax.broadcasted_iota(jnp.int32, (BLOCK, LANES), 1) < B_HEAD_DIM
    zero = jnp.zeros((n_keys, LANES), BF16)
    k_sel, v_sel = [], []
    for h in range(B_KV_HEADS):
        hs = slice(h * LANES, (h + 1) * LANES)
        k2 = jnp.concatenate([kp_ref[0, :, hs], kc_ref[0, :, hs], kn_ref[0, :, hs]], axis=0)
        v2 = jnp.concatenate([vp_ref[0, :, hs], vc_ref[0, :, hs], vn_ref[0, :, hs]], axis=0)
        k_sel.append((jnp.where(lo_k, k2, zero), jnp.where(lo_k, zero, k2)))
        v_sel.append((jnp.where(lo_k, v2, zero), jnp.where(lo_k, zero, v2)))

    for i in range(n_qb):
        variant = jnp.int32(1)
        if i == 0:
            variant = jnp.where(j == 0, 0, variant)
        if i == n_qb - 1:
            variant = jnp.where(j == last, 2, variant)
        rows = slice(i * BLOCK, (i + 1) * BLOCK)
        win = slice(i * BLOCK, (i + 3) * BLOCK)
        for p_idx in range(B_Q_HEADS // 2):
            qp = q_ref[0, rows, p_idx * LANES:(p_idx + 1) * LANES]
            for par in range(2):
                hq = p_idx * 2 + par
                k_win = k_sel[p_idx // 2][par][win]
                s_scr[i * B_Q_HEADS + hq] = _dot_nt(qp, k_win) + bias_ref[variant, hq]

    for i in range(n_qb):
        rows = slice(i * BLOCK, (i + 1) * BLOCK)
        win = slice(i * BLOCK, (i + 3) * BLOCK)
        for p_idx in range(B_Q_HEADS // 2):
            ps = slice(p_idx * LANES, (p_idx + 1) * LANES)
            probs, dens = [], []
            for par in range(2):
                hq = p_idx * 2 + par
                sink = sink_ref[hq] * LOG2E
                sc = s_scr[i * B_Q_HEADS + hq]
                m = jnp.maximum(jnp.max(sc, axis=-1, keepdims=True), sink)
                p = jnp.exp2(sc - m)
                dens.append(jnp.sum(p, axis=-1, keepdims=True) + jnp.exp2(sink - m))
                probs.append(p.astype(BF16))
            v_even, v_odd = v_sel[p_idx // 2]
            v_bd = jnp.concatenate([v_even[win], v_odd[win]], axis=0)
            out = _dot(jnp.concatenate(probs, axis=1), v_bd)
            y = out / jnp.where(lo_q, dens[0], dens[1])
            yb_scr[rows, ps] = (y * zb_ref[0, rows, ps].astype(F32)).astype(BF16)

    h = og_ref[0].astype(F32) * (hf_ref[0].astype(F32) + hb_ref[0].astype(F32))
    parts = []
    for k in range(A_HEADS):
        hs = slice(k * A_HEAD_DIM, (k + 1) * A_HEAD_DIM)
        hh = h[:, hs]
        hh = hh * lax.rsqrt(jnp.mean(hh * hh, axis=-1, keepdims=True) + EPS)
        parts.append(hh * mw_ref[:, hs])
    ya = (jnp.concatenate(parts, axis=1) * zg_ref[0].astype(F32)).astype(BF16)
    out = x_ref[0] + _dot(ya, wo_ref[:A_WIDTH, :]) + _dot(yb_scr[...], wo_ref[A_WIDTH:, :])
    if final:
        out = out * lax.rsqrt(jnp.mean(out * out, axis=-1, keepdims=True) + EPS) * fw_ref[...]
    o_ref[0] = out


def _attn_out(sink, qb, kb2, vb2, zb, bias, hf, hb, og, zg, x, mw, wo, fw, *, n_qb, final):
    bsz, seqlen, _ = qb.shape
    nb = seqlen // BLOCK
    assert nb >= 2 and nb % n_qb == 0
    tq = n_qb * BLOCK
    cur = lambda b, j: (b, j, 0)
    prev = lambda b, j: (b, jnp.maximum(j * n_qb - 1, 0), 0)
    nxt = lambda b, j: (b, jnp.minimum((j + 1) * n_qb, nb - 1), 0)
    halo = pl.BlockSpec((1, BLOCK, 2 * LANES), prev), pl.BlockSpec((1, BLOCK, 2 * LANES), nxt)
    kv_cur = pl.BlockSpec((1, tq, 2 * LANES), cur)

    def tok(width):
        return pl.BlockSpec((1, tq, width), cur)

    def full(arr):
        return pl.BlockSpec(arr.shape, lambda b, j: (0,) * arr.ndim)

    return pl.pallas_call(
        functools.partial(_attn_out_kernel, n_qb=n_qb, final=final),
        grid=(bsz, nb // n_qb),
        in_specs=[pl.BlockSpec(memory_space=pltpu.SMEM),
                  tok(B_WIDTH),
                  halo[0], kv_cur, halo[1],
                  halo[0], kv_cur, halo[1],
                  tok(B_WIDTH), full(bias),
                  tok(A_WIDTH), tok(A_WIDTH), tok(A_WIDTH), tok(A_WIDTH), tok(D_MODEL),
                  full(mw), full(wo), full(fw)],
        out_specs=tok(D_MODEL),
        out_shape=jax.ShapeDtypeStruct(x.shape, F32),
        scratch_shapes=[pltpu.VMEM((n_qb * B_Q_HEADS, BLOCK, 3 * BLOCK), F32),
                        pltpu.VMEM((tq, B_WIDTH), BF16)],
        compiler_params=_params("parallel", "parallel"),
        name="attn_out",
    )(sink, qb, kb2, kb2, kb2, vb2, vb2, vb2, zb, bias, hf, hb, og, zg, x, mw, wo, fw)


def _tile(n, target):
    t = min(n, target)
    assert n % t == 0, (n, t)
    return t


def kernel(x, norm_w, w_in, conv_w, conv_b, gate_b, mhn_w, sink, rel_bias, w_out, final_norm_w):
    bsz, seqlen, d_model = x.shape
    depth = norm_w.shape[0]
    assert d_model == D_MODEL and seqlen % CHUNK == 0
    tm = _tile(seqlen, 1024)
    tg = _tile(seqlen, 1024)
    n_qb = _tile(seqlen // BLOCK, 4)

    c_qk = 2 * A_WIDTH
    c_a = c_qk + 3 * A_WIDTH
    c_g = c_a + N_GATES
    wqk = w_in[:, :, :c_qk].astype(BF16)
    wa = w_in[:, :, c_qk:c_a].astype(BF16)
    wgt = jnp.swapaxes(w_in[:, :, c_a:c_g], 1, 2).astype(BF16)
    wb = w_in[:, :, c_g:].astype(BF16)
    wo = w_out.astype(BF16)
    cw = jnp.pad(conv_w, ((0, 0), (0, SUBLANES - CONV_K), (0, 0)))
    bias = _bias_table(rel_bias)
    fw = final_norm_w.reshape(1, D_MODEL)

    xf = x
    for l in range(depth):
        q, kt, va, og, zg, qb, kb2, vb2, zb, gr = _in_proj(
            xf, norm_w[l].reshape(1, D_MODEL), wqk[l], wa[l], wb[l], wgt[l], cw[l],
            conv_b[l].reshape(1, 2 * A_WIDTH), gate_b[l].reshape(N_GATES, 1), tm=tm)
        ar, br, cc, bc = _gate_prep(gr, tg=tg)
        hf, hb = _mlstm(q, kt, va, ar, br, cc, bc)
        xf = _attn_out(sink[l], qb, kb2, vb2, zb, bias, hf, hb, og, zg, xf,
                       mhn_w[l].reshape(1, A_WIDTH), wo[l], fw, n_qb=n_qb,
                       final=(l == depth - 1))
    return xf
```

```python
import functools
import math

import jax
import jax.numpy as jnp
import numpy as np
from jax import lax
from jax.experimental import pallas as pl
from jax.experimental.pallas import tpu as pltpu

D_MODEL = 1024
A_WIDTH = 512
A_HEADS = 4
A_HEAD_DIM = 128
CHUNK = 128
CONV_K = 5
B_WIDTH = 512
B_HEAD_DIM = 64
B_Q_HEADS = 8
B_KV_HEADS = 2
WINDOW = 128
BLOCK = 128
N_BUCKETS = 32
MAX_DISTANCE = 128
EPS = 1e-6
NEG_INF = -1e30
LOG2E = math.log2(math.e)
N_GATES = 4 * A_HEADS
N_CHAN = 2 * A_HEADS

LANES = 128
SUBLANES = 8
VMEM_LIMIT_BYTES = 56 * 1024 * 1024

HALO = 2 * SUBLANES
BF16 = jnp.bfloat16
F32 = jnp.float32


def _params(*sem):
    return pltpu.CompilerParams(dimension_semantics=sem, vmem_limit_bytes=VMEM_LIMIT_BYTES)


def _dot(a, b):
    return jnp.dot(a, b, preferred_element_type=F32)


def _dot_nt(a, b):
    return lax.dot_general(a, b, (((1,), (1,)), ((), ())), preferred_element_type=F32)


def _log_sigmoid(x):
    return -(jnp.maximum(-x, 0.0) + jnp.log1p(jnp.exp(-jnp.abs(x))))


def _silu(x):
    return x * jax.nn.sigmoid(x)


def _in_proj_kernel(x_ref, xp_ref, xn_ref, nw_ref, wqk_ref, wa_ref, wb_ref, wgt_ref,
                    cw_ref, cb_ref, gb_ref,
                    q_ref, kt_ref, va_ref, og_ref, zg_ref, qb_ref, kb2_ref, vb2_ref, zb_ref, gr_ref,
                    u_scr, *, tm):
    i = pl.program_id(1)
    last = pl.num_programs(1) - 1
    nw = nw_ref[...]

    def norm(xv):
        y = xv * lax.rsqrt(jnp.mean(xv * xv, axis=-1, keepdims=True) + EPS)
        return (y * nw).astype(BF16)

    hn = norm(x_ref[0])
    hp = norm(xp_ref[0])
    hx = norm(xn_ref[0])
    u_all = _dot(jnp.concatenate([hp, hn, hx], axis=0), wqk_ref[...])
    rr = lax.broadcasted_iota(jnp.int32, (tm + 2 * HALO, 1), 0)
    outside = ((rr < HALO) & (i == 0)) | ((rr >= HALO + tm) & (i == last))
    u_all = jnp.where(outside, 0.0, u_all)
    parts = []
    pad = CONV_K // 2
    for c in range(2 * A_WIDTH // LANES):
        cs = slice(c * LANES, (c + 1) * LANES)
        u_scr[c] = u_all[:, cs]
        acc = cb_ref[:, cs]
        for tap in range(CONV_K):
            lo = HALO - pad + tap
            acc = acc + u_scr[c, lo:lo + tm, :] * cw_ref[tap:tap + 1, cs]
        parts.append(acc)
    qk = _silu(jnp.concatenate(parts, axis=1))
    q_ref[0] = qk[:, :A_WIDTH].astype(BF16)
    k = qk[:, A_WIDTH:] * (A_HEAD_DIM ** -0.5)
    kt_ref[0] = k.T.astype(BF16)

    a = _dot(hn, wa_ref[...])
    va_ref[0] = a[:, :A_WIDTH].astype(BF16)
    og_ref[0] = jax.nn.sigmoid(a[:, A_WIDTH:2 * A_WIDTH]).astype(BF16)
    zg_ref[0] = _silu(a[:, 2 * A_WIDTH:]).astype(BF16)

    bq = _dot(hn, wb_ref[:, :B_WIDTH])
    qb_ref[0] = (bq * (B_HEAD_DIM ** -0.5 * LOG2E)).astype(BF16)
    kv = _dot(hn, wb_ref[:, B_WIDTH:B_WIDTH + 2 * LANES])
    half = lax.broadcasted_iota(jnp.int32, (tm, LANES), 1) < B_HEAD_DIM
    for src, dst in ((kv[:, :LANES], kb2_ref), (kv[:, LANES:], vb2_ref)):
        sw = pltpu.roll(src, B_HEAD_DIM, 1)
        dst[0, :, :LANES] = jnp.where(half, src, sw).astype(BF16)
        dst[0, :, LANES:] = jnp.where(half, sw, src).astype(BF16)
    zb_ref[0] = _silu(_dot(hn, wb_ref[:, B_WIDTH + 2 * LANES:])).astype(BF16)

    g = _dot_nt(wgt_ref[...], hn) + gb_ref[...]
    row = lax.broadcasted_iota(jnp.int32, g.shape, 0)
    gr_ref[0] = jnp.where(row < N_CHAN, g, _log_sigmoid(g))


def _in_proj(x, nw, wqk, wa, wb, wgt, cw, cb, gb, *, tm):
    bsz, seqlen, _ = x.shape
    nt = seqlen // tm
    hb = tm // HALO
    nhb = seqlen // HALO

    def full(arr):
        return pl.BlockSpec(arr.shape, lambda b, i: (0,) * arr.ndim)

    def rows(width):
        return pl.BlockSpec((1, tm, width), lambda b, i: (b, i, 0))

    out_shape = (
        jax.ShapeDtypeStruct((bsz, seqlen, A_WIDTH), BF16),
        jax.ShapeDtypeStruct((bsz, A_WIDTH, seqlen), BF16),
        jax.ShapeDtypeStruct((bsz, seqlen, A_WIDTH), BF16),
        jax.ShapeDtypeStruct((bsz, seqlen, A_WIDTH), BF16),
        jax.ShapeDtypeStruct((bsz, seqlen, A_WIDTH), BF16),
        jax.ShapeDtypeStruct((bsz, seqlen, B_WIDTH), BF16),
        jax.ShapeDtypeStruct((bsz, seqlen, 2 * LANES), BF16),
        jax.ShapeDtypeStruct((bsz, seqlen, 2 * LANES), BF16),
        jax.ShapeDtypeStruct((bsz, seqlen, B_WIDTH), BF16),
        jax.ShapeDtypeStruct((bsz, N_GATES, seqlen), F32),
    )
    out_specs = (
        rows(A_WIDTH),
        pl.BlockSpec((1, A_WIDTH, tm), lambda b, i: (b, 0, i)),
        rows(A_WIDTH), rows(A_WIDTH), rows(A_WIDTH), rows(B_WIDTH),
        rows(2 * LANES), rows(2 * LANES), rows(B_WIDTH),
        pl.BlockSpec((1, N_GATES, tm), lambda b, i: (b, 0, i)),
    )
    in_specs = [
        rows(D_MODEL),
        pl.BlockSpec((1, HALO, D_MODEL), lambda b, i: (b, jnp.maximum(i * hb - 1, 0), 0)),
        pl.BlockSpec((1, HALO, D_MODEL), lambda b, i: (b, jnp.minimum((i + 1) * hb, nhb - 1), 0)),
        full(nw), full(wqk), full(wa), full(wb), full(wgt), full(cw), full(cb), full(gb),
    ]
    return pl.pallas_call(
        functools.partial(_in_proj_kernel, tm=tm),
        grid=(bsz, nt),
        in_specs=in_specs,
        out_specs=out_specs,
        out_shape=out_shape,
        scratch_shapes=[pltpu.VMEM((2 * A_WIDTH // LANES, tm + 2 * HALO, LANES), F32)],
        compiler_params=_params("parallel", "arbitrary"),
        name="in_proj",
    )(x, x, x, nw, wqk, wa, wb, wgt, cw, cb, gb)


GROUP_CHUNKS = LANES // N_CHAN


def _gate_prep_kernel(gr_ref, ar_ref, br_ref, cpk_ref, bpk_ref):
    ti = lax.broadcasted_iota(jnp.int32, (CHUNK, CHUNK), 0)
    si = lax.broadcasted_iota(jnp.int32, (CHUNK, CHUNK), 1)
    upper_f = (ti <= si).astype(F32)
    lower_f = (ti >= si).astype(F32)
    fwd_row = lax.rem(ti, N_CHAN) < A_HEADS
    li = jnp.concatenate([gr_ref[0, :N_CHAN, c * CHUNK:(c + 1) * CHUNK]
                          for c in range(GROUP_CHUNKS)], axis=0) * LOG2E
    lf = jnp.concatenate([gr_ref[0, N_CHAN:, c * CHUNK:(c + 1) * CHUNK]
                          for c in range(GROUP_CHUNKS)], axis=0) * LOG2E
    b_pre = jnp.dot(lf, upper_f, preferred_element_type=F32, precision=lax.Precision.HIGHEST)
    b_suf = jnp.dot(lf, lower_f, preferred_element_type=F32, precision=lax.Precision.HIGHEST)
    b = jnp.where(fwd_row, b_pre, b_suf)
    a = li - b
    pre, suf = a, a
    sh = 1
    while sh < CHUNK:
        pre = jnp.where(si >= sh, jnp.maximum(pre, pltpu.roll(pre, sh, 1)), pre)
        suf = jnp.where(si < CHUNK - sh, jnp.maximum(suf, pltpu.roll(suf, CHUNK - sh, 1)), suf)
        sh *= 2
    cmax = jnp.where(fwd_row, pre, suf)
    for c in range(GROUP_CHUNKS):
        rs = slice(c * N_CHAN, (c + 1) * N_CHAN)
        ar_ref[0, :, c * CHUNK:(c + 1) * CHUNK] = a[rs, :]
        br_ref[0, :, c * CHUNK:(c + 1) * CHUNK] = b[rs, :]
    cpk_ref[0, 0] = cmax.T
    bpk_ref[0, 0] = b.T


def _gate_prep(gr):
    bsz, _, seqlen = gr.shape
    tg = GROUP_CHUNKS * CHUNK
    assert seqlen % tg == 0
    row_spec = pl.BlockSpec((1, N_CHAN, tg), lambda b, i: (b, 0, i))
    pk_spec = pl.BlockSpec((1, 1, CHUNK, LANES), lambda b, i: (b, i, 0, 0))
    pk_shape = jax.ShapeDtypeStruct((bsz, seqlen // tg, CHUNK, LANES), F32)
    return pl.pallas_call(
        _gate_prep_kernel,
        grid=(bsz, seqlen // tg),
        in_specs=[pl.BlockSpec((1, N_GATES, tg), lambda b, i: (b, 0, i))],
        out_specs=(row_spec, row_spec, pk_spec, pk_spec),
        out_shape=(jax.ShapeDtypeStruct((bsz, N_CHAN, seqlen), F32),
                   jax.ShapeDtypeStruct((bsz, N_CHAN, seqlen), F32),
                   pk_shape, pk_shape),
        compiler_params=_params("parallel", "parallel"),
        name="gate_prep",
    )(gr)


def _mlstm_kernel(qf_ref, qb_ref, ktf_ref, ktb_ref, vf_ref, vb_ref,
                  arf_ref, arb_ref, brf_ref, brb_ref, ccf_ref, ccb_ref, bcf_ref, bcb_ref,
                  hf_ref, hb_ref, c_scr, m_scr, *, bsz):
    j = pl.program_id(0)

    @pl.when(j == 0)
    def _():
        c_scr[...] = jnp.zeros_like(c_scr)
        m_scr[...] = jnp.zeros_like(m_scr)

    ti = lax.broadcasted_iota(jnp.int32, (CHUNK, CHUNK), 0)
    si = lax.broadcasted_iota(jnp.int32, (CHUNK, CHUNK), 1)
    ones_blk = jnp.ones((CHUNK, A_HEAD_DIM), BF16)
    sub8 = lax.broadcasted_iota(jnp.int32, (N_CHAN, LANES), 0)
    lane8 = lax.broadcasted_iota(jnp.int32, (N_CHAN, LANES), 1)

    nc = pl.num_programs(0)
    base_f = lax.rem(j, GROUP_CHUNKS) * N_CHAN
    base_b = lax.rem(nc - 1 - j, GROUP_CHUNKS) * N_CHAN
    dirs = (
        (0, qf_ref, ktf_ref, vf_ref, arf_ref, brf_ref, ccf_ref, bcf_ref, hf_ref, ti >= si, CHUNK - 1,
         base_f),
        (1, qb_ref, ktb_ref, vb_ref, arb_ref, brb_ref, ccb_ref, bcb_ref, hb_ref, ti <= si, 0,
         base_b),
    )

    def body(b, carry):
        tiles = []
        for d, q_ref, kt_ref, v_ref, ar_ref, br_ref, cc_ref, bc_ref, h_ref, vis, last, base in dirs:
            for h in range(A_HEADS):
                hs = slice(h * A_HEAD_DIM, (h + 1) * A_HEAD_DIM)
                idx = (b * 2 + d) * A_HEADS + h
                q = q_ref[b, :, hs]
                kt = kt_ref[b, hs, :]
                tiles.append((_dot(q, kt), q, kt, idx, hs))

        gates = []
        for d, q_ref, kt_ref, v_ref, ar_ref, br_ref, cc_ref, bc_ref, h_ref, vis, last, base in dirs:
            a_row = ar_ref[b]
            b_last = br_ref[b][:, last:last + 1]
            m8 = m_scr[b * 2 + d]
            gl8 = jnp.maximum(m8, jnp.max(a_row, axis=1, keepdims=True))
            w_row = jnp.exp2(a_row - gl8)
            decay8 = jnp.exp2(m8 - gl8)
            m_scr[b * 2 + d] = b_last + gl8
            m_lane = jnp.sum(jnp.where(sub8 == lane8, m8, 0.0), axis=0, keepdims=True)
            unrot = lax.rem(LANES - base, LANES)
            g = jnp.maximum(m_lane, pltpu.roll(cc_ref[b, 0], unrot, 1))
            emt = jnp.exp2(-(pltpu.roll(bc_ref[b, 0], unrot, 1) + g))
            gates.append((a_row, w_row, decay8, m8, g, emt))

        updates = []
        for d, q_ref, kt_ref, v_ref, ar_ref, br_ref, cc_ref, bc_ref, h_ref, vis, last, base in dirs:
            a_row, w_row, decay8, m8, g, emt = gates[d]
            for h in range(A_HEADS):
                ch = d * A_HEADS + h
                qk, q, kt, idx, hs = tiles[ch]
                c_old = c_scr[idx]
                v_aug = jnp.concatenate([v_ref[b, :, hs], ones_blk], axis=1)
                g_b = jnp.broadcast_to(g[:, ch:ch + 1], (CHUNK, LANES))
                p = jnp.where(vis, jnp.exp2(a_row[ch:ch + 1, :] - g_b), 0.0)
                iw_b = jnp.exp2(m8[ch:ch + 1, :] - g_b)
                lhs = jnp.concatenate([(qk * p).astype(BF16),
                                       (q.astype(F32) * iw_b).astype(BF16)], axis=1)
                rhs = jnp.concatenate([v_aug, c_old.astype(BF16)], axis=0)
                r = _dot(lhs, rhs)
                num = r[:, :A_HEAD_DIM]
                den = r[:, A_HEAD_DIM:]
                h_ref[b, :, hs] = (num / jnp.maximum(jnp.abs(den), emt[:, ch:ch + 1])).astype(BF16)
                ktw = (kt.astype(F32) * w_row[ch:ch + 1, :]).astype(BF16)
                updates.append((idx, decay8[ch:ch + 1, :1] * c_old, ktw, v_aug))

        for idx, c_dec, ktw, v_aug in updates:
            c_scr[idx] = c_dec + _dot(ktw, v_aug)
        return carry

    lax.fori_loop(0, bsz, body, 0, unroll=True)


def _mlstm(q, kt, va, ar, br, cc, bc):
    bsz, seqlen, _ = q.shape
    nc = seqlen // CHUNK
    fwd3 = lambda j: (0, j, 0)
    bwd3 = lambda j: (0, nc - 1 - j, 0)
    fwd3t = lambda j: (0, 0, j)
    bwd3t = lambda j: (0, 0, nc - 1 - j)
    tok = (bsz, CHUNK, A_WIDTH)
    tok_t = (bsz, A_WIDTH, CHUNK)
    rowb = (bsz, N_CHAN, CHUNK)
    colb = (bsz, 1, CHUNK, LANES)
    fwd4 = lambda j: (0, j // GROUP_CHUNKS, 0, 0)
    bwd4 = lambda j: (0, (nc - 1 - j) // GROUP_CHUNKS, 0, 0)
    in_specs = [
        pl.BlockSpec(tok, fwd3), pl.BlockSpec(tok, bwd3),
        pl.BlockSpec(tok_t, fwd3t), pl.BlockSpec(tok_t, bwd3t),
        pl.BlockSpec(tok, fwd3), pl.BlockSpec(tok, bwd3),
        pl.BlockSpec(rowb, fwd3t), pl.BlockSpec(rowb, bwd3t),
        pl.BlockSpec(rowb, fwd3t), pl.BlockSpec(rowb, bwd3t),
        pl.BlockSpec(colb, fwd4), pl.BlockSpec(colb, bwd4),
        pl.BlockSpec(colb, fwd4), pl.BlockSpec(colb, bwd4),
    ]
    return pl.pallas_call(
        functools.partial(_mlstm_kernel, bsz=bsz),
        grid=(nc,),
        in_specs=in_specs,
        out_specs=(pl.BlockSpec(tok, fwd3), pl.BlockSpec(tok, bwd3)),
        out_shape=(jax.ShapeDtypeStruct((bsz, seqlen, A_WIDTH), BF16),
                   jax.ShapeDtypeStruct((bsz, seqlen, A_WIDTH), BF16)),
        scratch_shapes=[pltpu.VMEM((bsz * 2 * A_HEADS, A_HEAD_DIM, 2 * A_HEAD_DIM), F32),
                        pltpu.VMEM((bsz * 2, N_CHAN, LANES), F32)],
        compiler_params=_params("arbitrary"),
        name="mlstm",
    )(q, q, kt, kt, va, va, ar, ar, br, br, cc, cc, bc, bc)


def _t5_bucket(rel):
    nb = N_BUCKETS // 2
    max_exact = nb // 2
    ret = jnp.where(rel > 0, nb, 0)
    n = jnp.abs(rel)
    nf = jnp.maximum(n, 1).astype(jnp.float32)
    large = max_exact + (jnp.log(nf / max_exact) / math.log(MAX_DISTANCE / max_exact)
                         * (nb - max_exact)).astype(jnp.int32)
    large = jnp.minimum(large, nb - 1)
    return ret + jnp.where(n < max_exact, n, large)


def _bias_kernel(rb_ref, bucket_ref, bias_ref):
    bucket = bucket_ref[...]
    qi = lax.broadcasted_iota(jnp.int32, bucket.shape, 0)
    kj = lax.broadcasted_iota(jnp.int32, bucket.shape, 1)
    band = jnp.abs(kj - BLOCK - qi) <= WINDOW
    masks = (band & (kj >= BLOCK), band, band & (kj < 2 * BLOCK))
    for hq in range(B_Q_HEADS):
        acc = jnp.zeros(bucket.shape, F32)
        for nb in range(N_BUCKETS):
            acc = jnp.where(bucket == nb, rb_ref[nb, hq], acc)
        acc = acc * LOG2E
        for v, mask in enumerate(masks):
            bias_ref[v, hq] = jnp.where(mask, acc, NEG_INF)


def _bias_table(rel_bias):
    q_off = jnp.arange(BLOCK)
    k_off = jnp.arange(3 * BLOCK) - BLOCK
    bucket = _t5_bucket(k_off[None, :] - q_off[:, None]).astype(jnp.int32)
    shape = (3, B_Q_HEADS, BLOCK, 3 * BLOCK)
    return pl.pallas_call(
        _bias_kernel,
        in_specs=[pl.BlockSpec(memory_space=pltpu.SMEM),
                  pl.BlockSpec(bucket.shape, lambda: (0, 0))],
        out_specs=pl.BlockSpec(shape, lambda: (0, 0, 0, 0)),
        out_shape=jax.ShapeDtypeStruct(shape, F32),
        name="bias_table",
    )(rel_bias.astype(F32), bucket)


def _attn_out_kernel(sink_ref, q_ref, kp_ref, kc_ref, kn_ref, vp_ref, vc_ref, vn_ref, zb_ref,
                     bias_ref, hf_ref, hb_ref, og_ref, zg_ref, x_ref, mw_ref, wo_ref, fw_ref,
                     o_ref, s_scr, yb_scr, *, n_qb, final):
    j = pl.program_id(1)
    last = pl.num_programs(1) - 1
    n_keys = (n_qb + 2) * BLOCK
    lo_k = lax.broadcasted_iota(jnp.int32, (n_keys, LANES), 1) < B_HEAD_DIM
    lo_q = lax.broadcasted_iota(jnp.int32, (BLOCK, LANES), 1) < B_HEAD_DIM
    zero = jnp.zeros((n_keys, LANES), BF16)
    k_sel, v_sel = [], []
    for h in range(B_KV_HEADS):
        hs = slice(h * LANES, (h + 1) * LANES)
        k2 = jnp.concatenate([kp_ref[0, :, hs], kc_ref[0, :, hs], kn_ref[0, :, hs]], axis=0)
        v2 = jnp.concatenate([vp_ref[0, :, hs], vc_ref[0, :, hs], vn_ref[0, :, hs]], axis=0)
        k_sel.append((jnp.where(lo_k, k2, zero), jnp.where(lo_k, zero, k2)))
        v_sel.append((jnp.where(lo_k, v2, zero), jnp.where(lo_k, zero, v2)))

    for i in range(n_qb):
        variant = jnp.int32(1)
        if i == 0:
            variant = jnp.where(j == 0, 0, variant)
        if i == n_qb - 1:
            variant = jnp.where(j == last, 2, variant)
        rows = slice(i * BLOCK, (i + 1) * BLOCK)
        win = slice(i * BLOCK, (i + 3) * BLOCK)
        for p_idx in range(B_Q_HEADS // 2):
            qp = q_ref[0, rows, p_idx * LANES:(p_idx + 1) * LANES]
            for par in range(2):
                hq = p_idx * 2 + par
                k_win = k_sel[p_idx // 2][par][win]
                s_scr[i * B_Q_HEADS + hq] = _dot_nt(qp, k_win) + bias_ref[variant, hq]

    for i in range(n_qb):
        rows = slice(i * BLOCK, (i + 1) * BLOCK)
        win = slice(i * BLOCK, (i + 3) * BLOCK)
        for p_idx in range(B_Q_HEADS // 2):
            ps = slice(p_idx * LANES, (p_idx + 1) * LANES)
            probs, dens = [], []
            for par in range(2):
                hq = p_idx * 2 + par
                sink = sink_ref[hq] * LOG2E
                sc = s_scr[i * B_Q_HEADS + hq]
                m = jnp.maximum(jnp.max(sc, axis=-1, keepdims=True), sink)
                p = jnp.exp2(sc - m)
                dens.append(jnp.sum(p, axis=-1, keepdims=True) + jnp.exp2(sink - m))
                probs.append(p.astype(BF16))
            v_even, v_odd = v_sel[p_idx // 2]
            v_bd = jnp.concatenate([v_even[win], v_odd[win]], axis=0)
            out = _dot(jnp.concatenate(probs, axis=1), v_bd)
            y = out / jnp.where(lo_q, dens[0], dens[1])
            yb_scr[rows, ps] = (y * zb_ref[0, rows, ps].astype(F32)).astype(BF16)

    h = og_ref[0].astype(F32) * (hf_ref[0].astype(F32) + hb_ref[0].astype(F32))
    parts = []
    for k in range(A_HEADS):
        hs = slice(k * A_HEAD_DIM, (k + 1) * A_HEAD_DIM)
        hh = h[:, hs]
        hh = hh * lax.rsqrt(jnp.mean(hh * hh, axis=-1, keepdims=True) + EPS)
        parts.append(hh * mw_ref[:, hs])
    ya = (jnp.concatenate(parts, axis=1) * zg_ref[0].astype(F32)).astype(BF16)
    out = x_ref[0] + _dot(ya, wo_ref[:A_WIDTH, :]) + _dot(yb_scr[...], wo_ref[A_WIDTH:, :])
    if final:
        out = out * lax.rsqrt(jnp.mean(out * out, axis=-1, keepdims=True) + EPS) * fw_ref[...]
    o_ref[0] = out


def _attn_out(sink, qb, kb2, vb2, zb, bias, hf, hb, og, zg, x, mw, wo, fw, *, n_qb, final):
    bsz, seqlen, _ = qb.shape
    nb = seqlen // BLOCK
    assert nb >= 2 and nb % n_qb == 0
    tq = n_qb * BLOCK
    cur = lambda b, j: (b, j, 0)
    prev = lambda b, j: (b, jnp.maximum(j * n_qb - 1, 0), 0)
    nxt = lambda b, j: (b, jnp.minimum((j + 1) * n_qb, nb - 1), 0)
    halo = pl.BlockSpec((1, BLOCK, 2 * LANES), prev), pl.BlockSpec((1, BLOCK, 2 * LANES), nxt)
    kv_cur = pl.BlockSpec((1, tq, 2 * LANES), cur)

    def tok(width):
        return pl.BlockSpec((1, tq, width), cur)

    def full(arr):
        return pl.BlockSpec(arr.shape, lambda b, j: (0,) * arr.ndim)

    return pl.pallas_call(
        functools.partial(_attn_out_kernel, n_qb=n_qb, final=final),
        grid=(bsz, nb // n_qb),
        in_specs=[pl.BlockSpec(memory_space=pltpu.SMEM),
                  tok(B_WIDTH),
                  halo[0], kv_cur, halo[1],
                  halo[0], kv_cur, halo[1],
                  tok(B_WIDTH), full(bias),
                  tok(A_WIDTH), tok(A_WIDTH), tok(A_WIDTH), tok(A_WIDTH), tok(D_MODEL),
                  full(mw), full(wo), full(fw)],
        out_specs=tok(D_MODEL),
        out_shape=jax.ShapeDtypeStruct(x.shape, F32),
        scratch_shapes=[pltpu.VMEM((n_qb * B_Q_HEADS, BLOCK, 3 * BLOCK), F32),
                        pltpu.VMEM((tq, B_WIDTH), BF16)],
        compiler_params=_params("parallel", "parallel"),
        name="attn_out",
    )(sink, qb, kb2, kb2, kb2, vb2, vb2, vb2, zb, bias, hf, hb, og, zg, x, mw, wo, fw)


def _tile(n, target):
    t = min(n, target)
    assert n % t == 0, (n, t)
    return t


def kernel(x, norm_w, w_in, conv_w, conv_b, gate_b, mhn_w, sink, rel_bias, w_out, final_norm_w):
    bsz, seqlen, d_model = x.shape
    depth = norm_w.shape[0]
    assert d_model == D_MODEL and seqlen % CHUNK == 0
    tm = _tile(seqlen, 1024)
    n_qb = _tile(seqlen // BLOCK, 4)

    c_qk = 2 * A_WIDTH
    c_a = c_qk + 3 * A_WIDTH
    c_g = c_a + N_GATES
    wqk = w_in[:, :, :c_qk].astype(BF16)
    wa = w_in[:, :, c_qk:c_a].astype(BF16)
    wgt = jnp.swapaxes(w_in[:, :, c_a:c_g], 1, 2).astype(BF16)
    wb = w_in[:, :, c_g:].astype(BF16)
    wo = w_out.astype(BF16)
    cw = jnp.pad(conv_w, ((0, 0), (0, SUBLANES - CONV_K), (0, 0)))
    bias = _bias_table(rel_bias)
    fw = final_norm_w.reshape(1, D_MODEL)

    xf = x
    for l in range(depth):
        q, kt, va, og, zg, qb, kb2, vb2, zb, gr = _in_proj(
            xf, norm_w[l].reshape(1, D_MODEL), wqk[l], wa[l], wb[l], wgt[l], cw[l],
            conv_b[l].reshape(1, 2 * A_WIDTH), gate_b[l].reshape(N_GATES, 1), tm=tm)
        ar, br, cc, bc = _gate_prep(gr)
        hf, hb = _mlstm(q, kt, va, ar, br, cc, bc)
        xf = _attn_out(sink[l], qb, kb2, vb2, zb, bias, hf, hb, og, zg, xf,
                       mhn_w[l].reshape(1, A_WIDTH), wo[l], fw, n_qb=n_qb,
                       final=(l == depth - 1))
    return xf
```

```python
import functools
import math

import jax
import jax.numpy as jnp
import numpy as np
from jax import lax
from jax.experimental import pallas as pl
from jax.experimental.pallas import tpu as pltpu

D_MODEL = 1024
A_WIDTH = 512
A_HEADS = 4
A_HEAD_DIM = 128
CHUNK = 128
CONV_K = 5
B_WIDTH = 512
B_HEAD_DIM = 64
B_Q_HEADS = 8
B_KV_HEADS = 2
WINDOW = 128
BLOCK = 128
N_BUCKETS = 32
MAX_DISTANCE = 128
EPS = 1e-6
NEG_INF = -1e30
LOG2E = math.log2(math.e)
N_GATES = 4 * A_HEADS
N_CHAN = 2 * A_HEADS

LANES = 128
SUBLANES = 8
VMEM_LIMIT_BYTES = 56 * 1024 * 1024

HALO = 2 * SUBLANES
BF16 = jnp.bfloat16
F32 = jnp.float32


def _params(*sem):
    return pltpu.CompilerParams(dimension_semantics=sem, vmem_limit_bytes=VMEM_LIMIT_BYTES)


def _dot(a, b):
    return jnp.dot(a, b, preferred_element_type=F32)


def _dot_nt(a, b):
    return lax.dot_general(a, b, (((1,), (1,)), ((), ())), preferred_element_type=F32)


def _log_sigmoid(x):
    return -(jnp.maximum(-x, 0.0) + jnp.log1p(jnp.exp(-jnp.abs(x))))


def _sigmoid(x):
    return 0.5 * jnp.tanh(0.5 * x) + 0.5


def _silu_of_half(h):
    return h + h * jnp.tanh(h)


def _layer_spec(arr, layer):
    return pl.BlockSpec((1,) + arr.shape[1:], lambda *_: (layer,) + (0,) * (arr.ndim - 1))


def _in_proj_kernel(x_ref, xp_ref, xn_ref, nw_ref, wqk_ref, wa_ref, wb_ref, wgt_ref,
                    cw_ref, cb_ref, gb_ref,
                    q_ref, kt_ref, va_ref, og_ref, zg_ref, qb_ref, kb2_ref, vb2_ref, zb_ref, gr_ref,
                    u_scr, *, tm):
    i = pl.program_id(1)
    last = pl.num_programs(1) - 1
    nw = nw_ref[0]

    def norm(xv):
        y = xv * lax.rsqrt(jnp.mean(xv * xv, axis=-1, keepdims=True) + EPS)
        return (y * nw).astype(BF16)

    hn = norm(x_ref[0])
    hp = norm(xp_ref[0])
    hx = norm(xn_ref[0])
    u_all = _dot(jnp.concatenate([hp, hn, hx], axis=0), wqk_ref[0])
    u_top = jnp.where(i == 0, 0.0, u_all[:HALO])
    u_bot = jnp.where(i == last, 0.0, u_all[HALO + tm:])
    cw_half = 0.5 * cw_ref[0]
    cb_half = 0.5 * cb_ref[0]
    parts = []
    pad = CONV_K // 2
    for c in range(2 * A_WIDTH // LANES):
        cs = slice(c * LANES, (c + 1) * LANES)
        u_scr[c, :HALO, :] = u_top[:, cs]
        u_scr[c, HALO:HALO + tm, :] = u_all[HALO:HALO + tm, cs]
        u_scr[c, HALO + tm:, :] = u_bot[:, cs]
        acc = cb_half[:, cs]
        for tap in range(CONV_K):
            lo = HALO - pad + tap
            acc = acc + u_scr[c, lo:lo + tm, :] * cw_half[tap:tap + 1, cs]
        parts.append(acc)
    qk = _silu_of_half(jnp.concatenate(parts, axis=1))
    q_ref[0] = qk[:, :A_WIDTH].astype(BF16)
    k = qk[:, A_WIDTH:] * (A_HEAD_DIM ** -0.5)
    kt_ref[0] = k.T.astype(BF16)

    a = _dot(hn, wa_ref[0])
    va_ref[0] = a[:, :A_WIDTH].astype(BF16)
    og_ref[0] = _sigmoid(a[:, A_WIDTH:2 * A_WIDTH]).astype(BF16)
    zg_ref[0] = _silu_of_half(0.5 * a[:, 2 * A_WIDTH:]).astype(BF16)

    bq = _dot(hn, wb_ref[0, :, :B_WIDTH])
    qb_ref[0] = (bq * (B_HEAD_DIM ** -0.5 * LOG2E)).astype(BF16)
    kv = _dot(hn, wb_ref[0, :, B_WIDTH:B_WIDTH + 2 * LANES])
    half = lax.broadcasted_iota(jnp.int32, (tm, LANES), 1) < B_HEAD_DIM
    for src, dst in ((kv[:, :LANES], kb2_ref), (kv[:, LANES:], vb2_ref)):
        sw = pltpu.roll(src, B_HEAD_DIM, 1)
        dst[0, :, :LANES] = jnp.where(half, src, sw).astype(BF16)
        dst[0, :, LANES:] = jnp.where(half, sw, src).astype(BF16)
    zb_ref[0] = _silu_of_half(0.5 * _dot(hn, wb_ref[0, :, B_WIDTH + 2 * LANES:])).astype(BF16)

    g = _dot_nt(wgt_ref[0], hn) + gb_ref[0]
    row = lax.broadcasted_iota(jnp.int32, g.shape, 0)
    gr_ref[0] = jnp.where(row < N_CHAN, g, _log_sigmoid(g))


def _in_proj(x, nw, wqk, wa, wb, wgt, cw, cb, gb, *, tm, layer):
    bsz, seqlen, _ = x.shape
    nt = seqlen // tm
    hb = tm // HALO
    nhb = seqlen // HALO

    def full(arr):
        return _layer_spec(arr, layer)

    def rows(width):
        return pl.BlockSpec((1, tm, width), lambda b, i: (b, i, 0))

    out_shape = (
        jax.ShapeDtypeStruct((bsz, seqlen, A_WIDTH), BF16),
        jax.ShapeDtypeStruct((bsz, A_WIDTH, seqlen), BF16),
        jax.ShapeDtypeStruct((bsz, seqlen, A_WIDTH), BF16),
        jax.ShapeDtypeStruct((bsz, seqlen, A_WIDTH), BF16),
        jax.ShapeDtypeStruct((bsz, seqlen, A_WIDTH), BF16),
        jax.ShapeDtypeStruct((bsz, seqlen, B_WIDTH), BF16),
        jax.ShapeDtypeStruct((bsz, seqlen, 2 * LANES), BF16),
        jax.ShapeDtypeStruct((bsz, seqlen, 2 * LANES), BF16),
        jax.ShapeDtypeStruct((bsz, seqlen, B_WIDTH), BF16),
        jax.ShapeDtypeStruct((bsz, N_GATES, seqlen), F32),
    )
    out_specs = (
        rows(A_WIDTH),
        pl.BlockSpec((1, A_WIDTH, tm), lambda b, i: (b, 0, i)),
        rows(A_WIDTH), rows(A_WIDTH), rows(A_WIDTH), rows(B_WIDTH),
        rows(2 * LANES), rows(2 * LANES), rows(B_WIDTH),
        pl.BlockSpec((1, N_GATES, tm), lambda b, i: (b, 0, i)),
    )
    in_specs = [
        rows(D_MODEL),
        pl.BlockSpec((1, HALO, D_MODEL), lambda b, i: (b, jnp.maximum(i * hb - 1, 0), 0)),
        pl.BlockSpec((1, HALO, D_MODEL), lambda b, i: (b, jnp.minimum((i + 1) * hb, nhb - 1), 0)),
        full(nw), full(wqk), full(wa), full(wb), full(wgt), full(cw), full(cb), full(gb),
    ]
    return pl.pallas_call(
        functools.partial(_in_proj_kernel, tm=tm),
        grid=(bsz, nt),
        in_specs=in_specs,
        out_specs=out_specs,
        out_shape=out_shape,
        scratch_shapes=[pltpu.VMEM((2 * A_WIDTH // LANES, tm + 2 * HALO, LANES), F32)],
        compiler_params=_params("parallel", "arbitrary"),
        name="in_proj",
    )(x, x, x, nw, wqk, wa, wb, wgt, cw, cb, gb)


GROUP_CHUNKS = LANES // N_CHAN


def _gate_prep_kernel(gr_ref, ar_ref, br_ref, cpk_ref, bpk_ref):
    ti = lax.broadcasted_iota(jnp.int32, (CHUNK, CHUNK), 0)
    si = lax.broadcasted_iota(jnp.int32, (CHUNK, CHUNK), 1)
    upper_f = (ti <= si).astype(F32)
    lower_f = (ti >= si).astype(F32)
    fwd_row = lax.rem(ti, N_CHAN) < A_HEADS
    li = jnp.concatenate([gr_ref[0, :N_CHAN, c * CHUNK:(c + 1) * CHUNK]
                          for c in range(GROUP_CHUNKS)], axis=0) * LOG2E
    lf = jnp.concatenate([gr_ref[0, N_CHAN:, c * CHUNK:(c + 1) * CHUNK]
                          for c in range(GROUP_CHUNKS)], axis=0) * LOG2E
    b_pre = jnp.dot(lf, upper_f, preferred_element_type=F32, precision=lax.Precision.HIGHEST)
    b_suf = jnp.dot(lf, lower_f, preferred_element_type=F32, precision=lax.Precision.HIGHEST)
    b = jnp.where(fwd_row, b_pre, b_suf)
    a = li - b
    pre, suf = a, a
    sh = 1
    while sh < CHUNK:
        pre = jnp.where(si >= sh, jnp.maximum(pre, pltpu.roll(pre, sh, 1)), pre)
        suf = jnp.where(si < CHUNK - sh, jnp.maximum(suf, pltpu.roll(suf, CHUNK - sh, 1)), suf)
        sh *= 2
    cmax = jnp.where(fwd_row, pre, suf)
    for c in range(GROUP_CHUNKS):
        rs = slice(c * N_CHAN, (c + 1) * N_CHAN)
        ar_ref[0, :, c * CHUNK:(c + 1) * CHUNK] = a[rs, :]
        br_ref[0, :, c * CHUNK:(c + 1) * CHUNK] = b[rs, :]
    cpk_ref[0, 0] = cmax.T
    bpk_ref[0, 0] = b.T


def _gate_prep(gr):
    bsz, _, seqlen = gr.shape
    tg = GROUP_CHUNKS * CHUNK
    assert seqlen % tg == 0
    row_spec = pl.BlockSpec((1, N_CHAN, tg), lambda b, i: (b, 0, i))
    pk_spec = pl.BlockSpec((1, 1, CHUNK, LANES), lambda b, i: (b, i, 0, 0))
    pk_shape = jax.ShapeDtypeStruct((bsz, seqlen // tg, CHUNK, LANES), F32)
    return pl.pallas_call(
        _gate_prep_kernel,
        grid=(bsz, seqlen // tg),
        in_specs=[pl.BlockSpec((1, N_GATES, tg), lambda b, i: (b, 0, i))],
        out_specs=(row_spec, row_spec, pk_spec, pk_spec),
        out_shape=(jax.ShapeDtypeStruct((bsz, N_CHAN, seqlen), F32),
                   jax.ShapeDtypeStruct((bsz, N_CHAN, seqlen), F32),
                   pk_shape, pk_shape),
        compiler_params=_params("parallel", "parallel"),
        name="gate_prep",
    )(gr)


def _mlstm_kernel(qf_ref, qb_ref, ktf_ref, ktb_ref, vf_ref, vb_ref,
                  arf_ref, arb_ref, brf_ref, brb_ref, ccf_ref, ccb_ref, bcf_ref, bcb_ref,
                  hf_ref, hb_ref, c_scr, m_scr, *, bsz):
    j = pl.program_id(0)

    @pl.when(j == 0)
    def _():
        c_scr[...] = jnp.zeros_like(c_scr)
        m_scr[...] = jnp.zeros_like(m_scr)

    ti = lax.broadcasted_iota(jnp.int32, (CHUNK, CHUNK), 0)
    si = lax.broadcasted_iota(jnp.int32, (CHUNK, CHUNK), 1)
    ones_blk = jnp.ones((CHUNK, A_HEAD_DIM), BF16)
    sub8 = lax.broadcasted_iota(jnp.int32, (N_CHAN, LANES), 0)
    lane8 = lax.broadcasted_iota(jnp.int32, (N_CHAN, LANES), 1)

    nc = pl.num_programs(0)
    base_f = lax.rem(j, GROUP_CHUNKS) * N_CHAN
    base_b = lax.rem(nc - 1 - j, GROUP_CHUNKS) * N_CHAN
    dirs = (
        (0, qf_ref, ktf_ref, vf_ref, arf_ref, brf_ref, ccf_ref, bcf_ref, hf_ref, ti >= si, CHUNK - 1,
         base_f),
        (1, qb_ref, ktb_ref, vb_ref, arb_ref, brb_ref, ccb_ref, bcb_ref, hb_ref, ti <= si, 0,
         base_b),
    )

    def body(b, carry):
        tiles = []
        for d, q_ref, kt_ref, v_ref, ar_ref, br_ref, cc_ref, bc_ref, h_ref, vis, last, base in dirs:
            for h in range(A_HEADS):
                hs = slice(h * A_HEAD_DIM, (h + 1) * A_HEAD_DIM)
                idx = (b * 2 + d) * A_HEADS + h
                q = q_ref[b, :, hs]
                kt = kt_ref[b, hs, :]
                tiles.append((_dot(q, kt), q, kt, idx, hs))

        gates = []
        for d, q_ref, kt_ref, v_ref, ar_ref, br_ref, cc_ref, bc_ref, h_ref, vis, last, base in dirs:
            a_row = ar_ref[b]
            b_last = br_ref[b][:, last:last + 1]
            m8 = m_scr[b * 2 + d]
            gl8 = jnp.maximum(m8, jnp.max(a_row, axis=1, keepdims=True))
            w_row = jnp.exp2(a_row - gl8)
            decay8 = jnp.exp2(m8 - gl8)
            m_scr[b * 2 + d] = b_last + gl8
            m_lane = jnp.sum(jnp.where(sub8 == lane8, m8, 0.0), axis=0, keepdims=True)
            unrot = lax.rem(LANES - base, LANES)
            g = jnp.maximum(m_lane, pltpu.roll(cc_ref[b, 0], unrot, 1))
            emt = jnp.exp2(-(pltpu.roll(bc_ref[b, 0], unrot, 1) + g))
            gates.append((a_row, w_row, decay8, m8, g, emt))

        updates = []
        for d, q_ref, kt_ref, v_ref, ar_ref, br_ref, cc_ref, bc_ref, h_ref, vis, last, base in dirs:
            a_row, w_row, decay8, m8, g, emt = gates[d]
            for h in range(A_HEADS):
                ch = d * A_HEADS + h
                qk, q, kt, idx, hs = tiles[ch]
                c_old = c_scr[idx]
                v_aug = jnp.concatenate([v_ref[b, :, hs], ones_blk], axis=1)
                g_b = jnp.broadcast_to(g[:, ch:ch + 1], (CHUNK, LANES))
                p = jnp.where(vis, jnp.exp2(a_row[ch:ch + 1, :] - g_b), 0.0)
                iw_b = jnp.exp2(m8[ch:ch + 1, :] - g_b)
                lhs = jnp.concatenate([(qk * p).astype(BF16),
                                       (q.astype(F32) * iw_b).astype(BF16)], axis=1)
                rhs = jnp.concatenate([v_aug, c_old.astype(BF16)], axis=0)
                r = _dot(lhs, rhs)
                num = r[:, :A_HEAD_DIM]
                den = r[:, A_HEAD_DIM:]
                h_ref[b, :, hs] = (num / jnp.maximum(jnp.abs(den), emt[:, ch:ch + 1])).astype(BF16)
                ktw = (kt.astype(F32) * w_row[ch:ch + 1, :]).astype(BF16)
                updates.append((idx, decay8[ch:ch + 1, :1] * c_old, ktw, v_aug))

        for idx, c_dec, ktw, v_aug in updates:
            c_scr[idx] = c_dec + _dot(ktw, v_aug)
        return carry

    lax.fori_loop(0, bsz, body, 0, unroll=True)


def _mlstm(q, kt, va, ar, br, cc, bc):
    bsz, seqlen, _ = q.shape
    nc = seqlen // CHUNK
    fwd3 = lambda j: (0, j, 0)
    bwd3 = lambda j: (0, nc - 1 - j, 0)
    fwd3t = lambda j: (0, 0, j)
    bwd3t = lambda j: (0, 0, nc - 1 - j)
    tok = (bsz, CHUNK, A_WIDTH)
    tok_t = (bsz, A_WIDTH, CHUNK)
    rowb = (bsz, N_CHAN, CHUNK)
    colb = (bsz, 1, CHUNK, LANES)
    fwd4 = lambda j: (0, j // GROUP_CHUNKS, 0, 0)
    bwd4 = lambda j: (0, (nc - 1 - j) // GROUP_CHUNKS, 0, 0)
    in_specs = [
        pl.BlockSpec(tok, fwd3), pl.BlockSpec(tok, bwd3),
        pl.BlockSpec(tok_t, fwd3t), pl.BlockSpec(tok_t, bwd3t),
        pl.BlockSpec(tok, fwd3), pl.BlockSpec(tok, bwd3),
        pl.BlockSpec(rowb, fwd3t), pl.BlockSpec(rowb, bwd3t),
        pl.BlockSpec(rowb, fwd3t), pl.BlockSpec(rowb, bwd3t),
        pl.BlockSpec(colb, fwd4), pl.BlockSpec(colb, bwd4),
        pl.BlockSpec(colb, fwd4), pl.BlockSpec(colb, bwd4),
    ]
    return pl.pallas_call(
        functools.partial(_mlstm_kernel, bsz=bsz),
        grid=(nc,),
        in_specs=in_specs,
        out_specs=(pl.BlockSpec(tok, fwd3), pl.BlockSpec(tok, bwd3)),
        out_shape=(jax.ShapeDtypeStruct((bsz, seqlen, A_WIDTH), BF16),
                   jax.ShapeDtypeStruct((bsz, seqlen, A_WIDTH), BF16)),
        scratch_shapes=[pltpu.VMEM((bsz * 2 * A_HEADS, A_HEAD_DIM, 2 * A_HEAD_DIM), F32),
                        pltpu.VMEM((bsz * 2, N_CHAN, LANES), F32)],
        compiler_params=_params("arbitrary"),
        name="mlstm",
    )(q, q, kt, kt, va, va, ar, ar, br, br, cc, cc, bc, bc)


def _t5_bucket(rel):
    nb = N_BUCKETS // 2
    max_exact = nb // 2
    ret = jnp.where(rel > 0, nb, 0)
    n = jnp.abs(rel)
    nf = jnp.maximum(n, 1).astype(jnp.float32)
    large = max_exact + (jnp.log(nf / max_exact) / math.log(MAX_DISTANCE / max_exact)
                         * (nb - max_exact)).astype(jnp.int32)
    large = jnp.minimum(large, nb - 1)
    return ret + jnp.where(n < max_exact, n, large)


def _bias_kernel(rb_ref, bucket_ref, bias_ref):
    bucket = bucket_ref[...]
    qi = lax.broadcasted_iota(jnp.int32, bucket.shape, 0)
    kj = lax.broadcasted_iota(jnp.int32, bucket.shape, 1)
    band = jnp.abs(kj - BLOCK - qi) <= WINDOW
    masks = (band & (kj >= BLOCK), band, band & (kj < 2 * BLOCK))
    for hq in range(B_Q_HEADS):
        acc = jnp.zeros(bucket.shape, F32)
        for nb in range(N_BUCKETS):
            acc = jnp.where(bucket == nb, rb_ref[nb, hq], acc)
        acc = acc * LOG2E
        for v, mask in enumerate(masks):
            bias_ref[v, hq] = jnp.where(mask, acc, NEG_INF)


def _bias_table(rel_bias):
    q_off = jnp.arange(BLOCK)
    k_off = jnp.arange(3 * BLOCK) - BLOCK
    bucket = _t5_bucket(k_off[None, :] - q_off[:, None]).astype(jnp.int32)
    shape = (3, B_Q_HEADS, BLOCK, 3 * BLOCK)
    return pl.pallas_call(
        _bias_kernel,
        in_specs=[pl.BlockSpec(memory_space=pltpu.SMEM),
                  pl.BlockSpec(bucket.shape, lambda: (0, 0))],
        out_specs=pl.BlockSpec(shape, lambda: (0, 0, 0, 0)),
        out_shape=jax.ShapeDtypeStruct(shape, F32),
        name="bias_table",
    )(rel_bias.astype(F32), bucket)


def _attn_out_kernel(sink_ref, q_ref, kp_ref, kc_ref, kn_ref, vp_ref, vc_ref, vn_ref, zb_ref,
                     bias_ref, hf_ref, hb_ref, og_ref, zg_ref, x_ref, mw_ref, wo_ref, fw_ref,
                     o_ref, s_scr, yb_scr, *, n_qb, layer, final):
    j = pl.program_id(1)
    last = pl.num_programs(1) - 1
    n_keys = (n_qb + 2) * BLOCK
    lo_k = lax.broadcasted_iota(jnp.int32, (n_keys, LANES), 1) < B_HEAD_DIM
    lo_q = lax.broadcasted_iota(jnp.int32, (BLOCK, LANES), 1) < B_HEAD_DIM
    zero = jnp.zeros((n_keys, LANES), BF16)
    k_sel, v_sel = [], []
    for h in range(B_KV_HEADS):
        hs = slice(h * LANES, (h + 1) * LANES)
        k2 = jnp.concatenate([kp_ref[0, :, hs], kc_ref[0, :, hs], kn_ref[0, :, hs]], axis=0)
        v2 = jnp.concatenate([vp_ref[0, :, hs], vc_ref[0, :, hs], vn_ref[0, :, hs]], axis=0)
        k_sel.append((jnp.where(lo_k, k2, zero), jnp.where(lo_k, zero, k2)))
        v_sel.append((jnp.where(lo_k, v2, zero), jnp.where(lo_k, zero, v2)))

    for i in range(n_qb):
        variant = jnp.int32(1)
        if i == 0:
            variant = jnp.where(j == 0, 0, variant)
        if i == n_qb - 1:
            variant = jnp.where(j == last, 2, variant)
        rows = slice(i * BLOCK, (i + 1) * BLOCK)
        win = slice(i * BLOCK, (i + 3) * BLOCK)
        for p_idx in range(B_Q_HEADS // 2):
            qp = q_ref[0, rows, p_idx * LANES:(p_idx + 1) * LANES]
            for par in range(2):
                hq = p_idx * 2 + par
                k_win = k_sel[p_idx // 2][par][win]
                s_scr[i * B_Q_HEADS + hq] = _dot_nt(qp, k_win) + bias_ref[variant, hq]

    for i in range(n_qb):
        rows = slice(i * BLOCK, (i + 1) * BLOCK)
        win = slice(i * BLOCK, (i + 3) * BLOCK)
        for p_idx in range(B_Q_HEADS // 2):
            ps = slice(p_idx * LANES, (p_idx + 1) * LANES)
            probs, dens = [], []
            for par in range(2):
                hq = p_idx * 2 + par
                sink = sink_ref[layer, hq] * LOG2E
                sc = s_scr[i * B_Q_HEADS + hq]
                m = jnp.maximum(jnp.max(sc, axis=-1, keepdims=True), sink)
                p = jnp.exp2(sc - m)
                dens.append(jnp.sum(p, axis=-1, keepdims=True) + jnp.exp2(sink - m))
                probs.append(p.astype(BF16))
            v_even, v_odd = v_sel[p_idx // 2]
            v_bd = jnp.concatenate([v_even[win], v_odd[win]], axis=0)
            out = _dot(jnp.concatenate(probs, axis=1), v_bd)
            y = out / jnp.where(lo_q, dens[0], dens[1])
            yb_scr[rows, ps] = (y * zb_ref[0, rows, ps].astype(F32)).astype(BF16)

    h = og_ref[0].astype(F32) * (hf_ref[0].astype(F32) + hb_ref[0].astype(F32))
    parts = []
    for k in range(A_HEADS):
        hs = slice(k * A_HEAD_DIM, (k + 1) * A_HEAD_DIM)
        hh = h[:, hs]
        hh = hh * lax.rsqrt(jnp.mean(hh * hh, axis=-1, keepdims=True) + EPS)
        parts.append(hh * mw_ref[0, :, hs])
    ya = (jnp.concatenate(parts, axis=1) * zg_ref[0].astype(F32)).astype(BF16)
    out = x_ref[0] + _dot(ya, wo_ref[0, :A_WIDTH, :]) + _dot(yb_scr[...], wo_ref[0, A_WIDTH:, :])
    if final:
        out = out * lax.rsqrt(jnp.mean(out * out, axis=-1, keepdims=True) + EPS) * fw_ref[...]
    o_ref[0] = out


def _attn_out(sink, qb, kb2, vb2, zb, bias, hf, hb, og, zg, x, mw, wo, fw, *, n_qb, layer, final):
    bsz, seqlen, _ = qb.shape
    nb = seqlen // BLOCK
    assert nb >= 2 and nb % n_qb == 0
    tq = n_qb * BLOCK
    cur = lambda b, j: (b, j, 0)
    prev = lambda b, j: (b, jnp.maximum(j * n_qb - 1, 0), 0)
    nxt = lambda b, j: (b, jnp.minimum((j + 1) * n_qb, nb - 1), 0)
    halo = pl.BlockSpec((1, BLOCK, 2 * LANES), prev), pl.BlockSpec((1, BLOCK, 2 * LANES), nxt)
    kv_cur = pl.BlockSpec((1, tq, 2 * LANES), cur)

    def tok(width):
        return pl.BlockSpec((1, tq, width), cur)

    def full(arr):
        return pl.BlockSpec(arr.shape, lambda b, j: (0,) * arr.ndim)

    return pl.pallas_call(
        functools.partial(_attn_out_kernel, n_qb=n_qb, layer=layer, final=final),
        grid=(bsz, nb // n_qb),
        in_specs=[pl.BlockSpec(memory_space=pltpu.SMEM),
                  tok(B_WIDTH),
                  halo[0], kv_cur, halo[1],
                  halo[0], kv_cur, halo[1],
                  tok(B_WIDTH), full(bias),
                  tok(A_WIDTH), tok(A_WIDTH), tok(A_WIDTH), tok(A_WIDTH), tok(D_MODEL),
                  _layer_spec(mw, layer), _layer_spec(wo, layer), full(fw)],
        out_specs=tok(D_MODEL),
        out_shape=jax.ShapeDtypeStruct(x.shape, F32),
        scratch_shapes=[pltpu.VMEM((n_qb * B_Q_HEADS, BLOCK, 3 * BLOCK), F32),
                        pltpu.VMEM((tq, B_WIDTH), BF16)],
        compiler_params=_params("parallel", "parallel"),
        name="attn_out",
    )(sink, qb, kb2, kb2, kb2, vb2, vb2, vb2, zb, bias, hf, hb, og, zg, x, mw, wo, fw)


def _tile(n, target):
    t = min(n, target)
    assert n % t == 0, (n, t)
    return t


def kernel(x, norm_w, w_in, conv_w, conv_b, gate_b, mhn_w, sink, rel_bias, w_out, final_norm_w):
    bsz, seqlen, d_model = x.shape
    depth = norm_w.shape[0]
    assert d_model == D_MODEL and seqlen % CHUNK == 0
    tm = _tile(seqlen, 1024)
    n_qb = _tile(seqlen // BLOCK, 4)

    c_qk = 2 * A_WIDTH
    c_a = c_qk + 3 * A_WIDTH
    c_g = c_a + N_GATES
    w_bf = w_in.astype(BF16)
    wqk = w_bf[:, :, :c_qk]
    wa = w_bf[:, :, c_qk:c_a]
    wgt = jnp.swapaxes(w_bf[:, :, c_a:c_g], 1, 2)
    wb = w_bf[:, :, c_g:]
    wo = w_out.astype(BF16)
    cw = jnp.pad(conv_w, ((0, 0), (0, SUBLANES - CONV_K), (0, 0)))
    nw = norm_w.reshape(depth, 1, D_MODEL)
    cb = conv_b.reshape(depth, 1, 2 * A_WIDTH)
    gb = gate_b.reshape(depth, N_GATES, 1)
    mw = mhn_w.reshape(depth, 1, A_WIDTH)
    bias = _bias_table(rel_bias)
    fw = final_norm_w.reshape(1, D_MODEL)

    xf = x
    for l in range(depth):
        q, kt, va, og, zg, qb, kb2, vb2, zb, gr = _in_proj(
            xf, nw, wqk, wa, wb, wgt, cw, cb, gb, tm=tm, layer=l)
        ar, br, cc, bc = _gate_prep(gr)
        hf, hb = _mlstm(q, kt, va, ar, br, cc, bc)
        xf = _attn_out(sink, qb, kb2, vb2, zb, bias, hf, hb, og, zg, xf, mw, wo, fw,
                       n_qb=n_qb, layer=l, final=(l == depth - 1))
    return xf
```

```python
import functools
import math

import jax
import jax.numpy as jnp
import numpy as np
from jax import lax
from jax.experimental import pallas as pl
from jax.experimental.pallas import tpu as pltpu

D_MODEL = 1024
A_WIDTH = 512
A_HEADS = 4
A_HEAD_DIM = 128
CHUNK = 128
CONV_K = 5
B_WIDTH = 512
B_HEAD_DIM = 64
B_Q_HEADS = 8
B_KV_HEADS = 2
WINDOW = 128
BLOCK = 128
N_BUCKETS = 32
MAX_DISTANCE = 128
EPS = 1e-6
NEG_INF = -1e30
LOG2E = math.log2(math.e)
N_GATES = 4 * A_HEADS
N_CHAN = 2 * A_HEADS

LANES = 128
SUBLANES = 8
VMEM_LIMIT_BYTES = 56 * 1024 * 1024

HALO = 2 * SUBLANES
BF16 = jnp.bfloat16
F32 = jnp.float32


def _params(*sem):
    return pltpu.CompilerParams(dimension_semantics=sem, vmem_limit_bytes=VMEM_LIMIT_BYTES)


def _dot(a, b):
    return jnp.dot(a, b, preferred_element_type=F32)


def _dot_nt(a, b):
    return lax.dot_general(a, b, (((1,), (1,)), ((), ())), preferred_element_type=F32)


def _log_sigmoid(x):
    return -(jnp.maximum(-x, 0.0) + jnp.log1p(jnp.exp(-jnp.abs(x))))


def _sigmoid(x):
    return 0.5 * jnp.tanh(0.5 * x) + 0.5


def _silu_of_half(h):
    return h + h * jnp.tanh(h)


def _layer_spec(arr, layer):
    return pl.BlockSpec((1,) + arr.shape[1:], lambda *_: (layer,) + (0,) * (arr.ndim - 1))


def _in_proj_kernel(x_ref, xp_ref, xn_ref, nw_ref, wqk_ref, wv_ref, wo_ref, wz_ref, wgt_ref, wb_ref,
                    cw_ref, cb_ref, gb_ref,
                    q_ref, kt_ref, va_ref, og_ref, zg_ref, qb_ref, kb2_ref, vb2_ref, zb_ref, gr_ref,
                    u_scr, *, tm):
    i = pl.program_id(1)
    last = pl.num_programs(1) - 1
    nw = nw_ref[0]

    def norm(xv):
        y = xv * lax.rsqrt(jnp.mean(xv * xv, axis=-1, keepdims=True) + EPS)
        return (y * nw).astype(BF16)

    hn = norm(x_ref[0])
    hp = norm(xp_ref[0])
    hx = norm(xn_ref[0])
    u_all = _dot(jnp.concatenate([hp, hn, hx], axis=0), wqk_ref[0])
    u_top = jnp.where(i == 0, 0.0, u_all[:HALO])
    u_bot = jnp.where(i == last, 0.0, u_all[HALO + tm:])
    cw_half = 0.5 * cw_ref[0]
    cb_half = 0.5 * cb_ref[0]
    parts = []
    pad = CONV_K // 2
    for c in range(2 * A_WIDTH // LANES):
        cs = slice(c * LANES, (c + 1) * LANES)
        u_scr[c, :HALO, :] = u_top[:, cs]
        u_scr[c, HALO:HALO + tm, :] = u_all[HALO:HALO + tm, cs]
        u_scr[c, HALO + tm:, :] = u_bot[:, cs]
        acc = cb_half[:, cs]
        for tap in range(CONV_K):
            lo = HALO - pad + tap
            acc = acc + u_scr[c, lo:lo + tm, :] * cw_half[tap:tap + 1, cs]
        parts.append(acc)
    qk = _silu_of_half(jnp.concatenate(parts, axis=1))
    q_ref[0] = qk[:, :A_WIDTH].astype(BF16)
    k = qk[:, A_WIDTH:] * (A_HEAD_DIM ** -0.5)
    kt_ref[0] = k.T.astype(BF16)

    va_ref[0] = _dot(hn, wv_ref[0]).astype(BF16)
    og_ref[0] = _sigmoid(_dot(hn, wo_ref[0])).astype(BF16)
    zg_ref[0] = _silu_of_half(0.5 * _dot(hn, wz_ref[0])).astype(BF16)

    bq = _dot(hn, wb_ref[0, :, :B_WIDTH])
    qb_ref[0] = (bq * (B_HEAD_DIM ** -0.5 * LOG2E)).astype(BF16)
    kv = _dot(hn, wb_ref[0, :, B_WIDTH:B_WIDTH + 2 * LANES])
    half = lax.broadcasted_iota(jnp.int32, (tm, LANES), 1) < B_HEAD_DIM
    for src, dst in ((kv[:, :LANES], kb2_ref), (kv[:, LANES:], vb2_ref)):
        sw = pltpu.roll(src, B_HEAD_DIM, 1)
        dst[0, :, :LANES] = jnp.where(half, src, sw).astype(BF16)
        dst[0, :, LANES:] = jnp.where(half, sw, src).astype(BF16)
    zb_ref[0] = _silu_of_half(0.5 * _dot(hn, wb_ref[0, :, B_WIDTH + 2 * LANES:])).astype(BF16)

    g = _dot_nt(wgt_ref[0], hn) + gb_ref[0]
    row = lax.broadcasted_iota(jnp.int32, g.shape, 0)
    gr_ref[0] = jnp.where(row < N_CHAN, g, _log_sigmoid(g))


def _in_proj(x, nw, w_all, wgt, wb, cw, cb, gb, *, tm, layer):
    bsz, seqlen, _ = x.shape
    nt = seqlen // tm
    hb = tm // HALO
    nhb = seqlen // HALO

    def full(arr):
        return _layer_spec(arr, layer)

    def w_cols(start, width):
        assert start % width == 0 and width % LANES == 0
        return pl.BlockSpec((1, D_MODEL, width), lambda b, i: (layer, 0, start // width))

    def rows(width):
        return pl.BlockSpec((1, tm, width), lambda b, i: (b, i, 0))

    out_shape = (
        jax.ShapeDtypeStruct((bsz, seqlen, A_WIDTH), BF16),
        jax.ShapeDtypeStruct((bsz, A_WIDTH, seqlen), BF16),
        jax.ShapeDtypeStruct((bsz, seqlen, A_WIDTH), BF16),
        jax.ShapeDtypeStruct((bsz, seqlen, A_WIDTH), BF16),
        jax.ShapeDtypeStruct((bsz, seqlen, A_WIDTH), BF16),
        jax.ShapeDtypeStruct((bsz, seqlen, B_WIDTH), BF16),
        jax.ShapeDtypeStruct((bsz, seqlen, 2 * LANES), BF16),
        jax.ShapeDtypeStruct((bsz, seqlen, 2 * LANES), BF16),
        jax.ShapeDtypeStruct((bsz, seqlen, B_WIDTH), BF16),
        jax.ShapeDtypeStruct((bsz, N_GATES, seqlen), F32),
    )
    out_specs = (
        rows(A_WIDTH),
        pl.BlockSpec((1, A_WIDTH, tm), lambda b, i: (b, 0, i)),
        rows(A_WIDTH), rows(A_WIDTH), rows(A_WIDTH), rows(B_WIDTH),
        rows(2 * LANES), rows(2 * LANES), rows(B_WIDTH),
        pl.BlockSpec((1, N_GATES, tm), lambda b, i: (b, 0, i)),
    )
    in_specs = [
        rows(D_MODEL),
        pl.BlockSpec((1, HALO, D_MODEL), lambda b, i: (b, jnp.maximum(i * hb - 1, 0), 0)),
        pl.BlockSpec((1, HALO, D_MODEL), lambda b, i: (b, jnp.minimum((i + 1) * hb, nhb - 1), 0)),
        full(nw),
        w_cols(0, 2 * A_WIDTH),
        w_cols(2 * A_WIDTH, A_WIDTH),
        w_cols(3 * A_WIDTH, A_WIDTH),
        w_cols(4 * A_WIDTH, A_WIDTH),
        full(wgt), full(wb), full(cw), full(cb), full(gb),
    ]
    return pl.pallas_call(
        functools.partial(_in_proj_kernel, tm=tm),
        grid=(bsz, nt),
        in_specs=in_specs,
        out_specs=out_specs,
        out_shape=out_shape,
        scratch_shapes=[pltpu.VMEM((2 * A_WIDTH // LANES, tm + 2 * HALO, LANES), F32)],
        compiler_params=_params("parallel", "arbitrary"),
        name="in_proj",
    )(x, x, x, nw, w_all, w_all, w_all, w_all, wgt, wb, cw, cb, gb)


GROUP_CHUNKS = LANES // N_CHAN


def _gate_prep_kernel(gr_ref, ar_ref, br_ref, cpk_ref, bpk_ref):
    ti = lax.broadcasted_iota(jnp.int32, (CHUNK, CHUNK), 0)
    si = lax.broadcasted_iota(jnp.int32, (CHUNK, CHUNK), 1)
    upper_f = (ti <= si).astype(F32)
    lower_f = (ti >= si).astype(F32)
    fwd_row = lax.rem(ti, N_CHAN) < A_HEADS
    li = jnp.concatenate([gr_ref[0, :N_CHAN, c * CHUNK:(c + 1) * CHUNK]
                          for c in range(GROUP_CHUNKS)], axis=0) * LOG2E
    lf = jnp.concatenate([gr_ref[0, N_CHAN:, c * CHUNK:(c + 1) * CHUNK]
                          for c in range(GROUP_CHUNKS)], axis=0) * LOG2E
    b_pre = jnp.dot(lf, upper_f, preferred_element_type=F32, precision=lax.Precision.HIGHEST)
    b_suf = jnp.dot(lf, lower_f, preferred_element_type=F32, precision=lax.Precision.HIGHEST)
    b = jnp.where(fwd_row, b_pre, b_suf)
    a = li - b
    pre, suf = a, a
    sh = 1
    while sh < CHUNK:
        pre = jnp.where(si >= sh, jnp.maximum(pre, pltpu.roll(pre, sh, 1)), pre)
        suf = jnp.where(si < CHUNK - sh, jnp.maximum(suf, pltpu.roll(suf, CHUNK - sh, 1)), suf)
        sh *= 2
    cmax = jnp.where(fwd_row, pre, suf)
    for c in range(GROUP_CHUNKS):
        rs = slice(c * N_CHAN, (c + 1) * N_CHAN)
        ar_ref[0, :, c * CHUNK:(c + 1) * CHUNK] = a[rs, :]
        br_ref[0, :, c * CHUNK:(c + 1) * CHUNK] = b[rs, :]
    cpk_ref[0, 0] = cmax.T
    bpk_ref[0, 0] = b.T


def _gate_prep(gr):
    bsz, _, seqlen = gr.shape
    tg = GROUP_CHUNKS * CHUNK
    assert seqlen % tg == 0
    row_spec = pl.BlockSpec((1, N_CHAN, tg), lambda b, i: (b, 0, i))
    pk_spec = pl.BlockSpec((1, 1, CHUNK, LANES), lambda b, i: (b, i, 0, 0))
    pk_shape = jax.ShapeDtypeStruct((bsz, seqlen // tg, CHUNK, LANES), F32)
    return pl.pallas_call(
        _gate_prep_kernel,
        grid=(bsz, seqlen // tg),
        in_specs=[pl.BlockSpec((1, N_GATES, tg), lambda b, i: (b, 0, i))],
        out_specs=(row_spec, row_spec, pk_spec, pk_spec),
        out_shape=(jax.ShapeDtypeStruct((bsz, N_CHAN, seqlen), F32),
                   jax.ShapeDtypeStruct((bsz, N_CHAN, seqlen), F32),
                   pk_shape, pk_shape),
        compiler_params=_params("parallel", "parallel"),
        name="gate_prep",
    )(gr)


CHUNKS_PER_STEP = 2


def _mlstm_kernel(qf_ref, qb_ref, ktf_ref, ktb_ref, vf_ref, vb_ref,
                  arf_ref, arb_ref, brf_ref, brb_ref, ccf_ref, ccb_ref, bcf_ref, bcb_ref,
                  hf_ref, hb_ref, c_scr, m_scr, *, bsz):
    j = pl.program_id(0)

    @pl.when(j == 0)
    def _():
        c_scr[...] = jnp.zeros_like(c_scr)
        m_scr[...] = jnp.zeros_like(m_scr)

    ti = lax.broadcasted_iota(jnp.int32, (CHUNK, CHUNK), 0)
    si = lax.broadcasted_iota(jnp.int32, (CHUNK, CHUNK), 1)
    ones_blk = jnp.ones((CHUNK, A_HEAD_DIM), BF16)
    sub8 = lax.broadcasted_iota(jnp.int32, (N_CHAN, LANES), 0)
    lane8 = lax.broadcasted_iota(jnp.int32, (N_CHAN, LANES), 1)

    n_steps = pl.num_programs(0)
    for k in range(CHUNKS_PER_STEP):
        _mlstm_chunk(j, k, n_steps, qf_ref, qb_ref, ktf_ref, ktb_ref, vf_ref, vb_ref,
                     arf_ref, arb_ref, brf_ref, brb_ref, ccf_ref, ccb_ref, bcf_ref, bcb_ref,
                     hf_ref, hb_ref, c_scr, m_scr, bsz, ti, si, ones_blk, sub8, lane8)


def _mlstm_chunk(j, k, n_steps, qf_ref, qb_ref, ktf_ref, ktb_ref, vf_ref, vb_ref,
                 arf_ref, arb_ref, brf_ref, brb_ref, ccf_ref, ccb_ref, bcf_ref, bcb_ref,
                 hf_ref, hb_ref, c_scr, m_scr, bsz, ti, si, ones_blk, sub8, lane8):
    jf = j * CHUNKS_PER_STEP + k
    jb = n_steps * CHUNKS_PER_STEP - 1 - jf
    base_f = lax.rem(jf, GROUP_CHUNKS) * N_CHAN
    base_b = lax.rem(jb, GROUP_CHUNKS) * N_CHAN
    rows_f = slice(k * CHUNK, (k + 1) * CHUNK)
    rows_b = slice((CHUNKS_PER_STEP - 1 - k) * CHUNK, (CHUNKS_PER_STEP - k) * CHUNK)
    dirs = (
        (0, qf_ref, ktf_ref, vf_ref, arf_ref, brf_ref, ccf_ref, bcf_ref, hf_ref, ti >= si, CHUNK - 1,
         base_f, rows_f),
        (1, qb_ref, ktb_ref, vb_ref, arb_ref, brb_ref, ccb_ref, bcb_ref, hb_ref, ti <= si, 0,
         base_b, rows_b),
    )

    def body(b, carry):
        tiles = []
        for d, q_ref, kt_ref, v_ref, ar_ref, br_ref, cc_ref, bc_ref, h_ref, vis, last, base, rows in dirs:
            for h in range(A_HEADS):
                hs = slice(h * A_HEAD_DIM, (h + 1) * A_HEAD_DIM)
                idx = (b * 2 + d) * A_HEADS + h
                q = q_ref[b, rows, hs]
                kt = kt_ref[b, hs, rows]
                tiles.append((_dot(q, kt), q, kt, idx, hs))

        gates = []
        for d, q_ref, kt_ref, v_ref, ar_ref, br_ref, cc_ref, bc_ref, h_ref, vis, last, base, rows in dirs:
            a_row = ar_ref[b, :, rows]
            b_last = br_ref[b, :, rows][:, last:last + 1]
            m8 = m_scr[b * 2 + d]
            gl8 = jnp.maximum(m8, jnp.max(a_row, axis=1, keepdims=True))
            w_row = jnp.exp2(a_row - gl8)
            decay8 = jnp.exp2(m8 - gl8)
            m_scr[b * 2 + d] = b_last + gl8
            m_lane = jnp.sum(jnp.where(sub8 == lane8, m8, 0.0), axis=0, keepdims=True)
            unrot = lax.rem(LANES - base, LANES)
            g = jnp.maximum(m_lane, pltpu.roll(cc_ref[b, 0], unrot, 1))
            emt = jnp.exp2(-(pltpu.roll(bc_ref[b, 0], unrot, 1) + g))
            gates.append((a_row, w_row, decay8, m8, g, emt))

        updates = []
        for d, q_ref, kt_ref, v_ref, ar_ref, br_ref, cc_ref, bc_ref, h_ref, vis, last, base, rows in dirs:
            a_row, w_row, decay8, m8, g, emt = gates[d]
            for h in range(A_HEADS):
                ch = d * A_HEADS + h
                qk, q, kt, idx, hs = tiles[ch]
                c_old = c_scr[idx]
                v_aug = jnp.concatenate([v_ref[b, rows, hs], ones_blk], axis=1)
                g_b = jnp.broadcast_to(g[:, ch:ch + 1], (CHUNK, LANES))
                p = jnp.where(vis, jnp.exp2(a_row[ch:ch + 1, :] - g_b), 0.0)
                iw_b = jnp.exp2(m8[ch:ch + 1, :] - g_b)
                lhs = jnp.concatenate([(qk * p).astype(BF16),
                                       (q.astype(F32) * iw_b).astype(BF16)], axis=1)
                rhs = jnp.concatenate([v_aug, c_old.astype(BF16)], axis=0)
                r = _dot(lhs, rhs)
                num = r[:, :A_HEAD_DIM]
                den = r[:, A_HEAD_DIM:]
                h_ref[b, rows, hs] = (num / jnp.maximum(jnp.abs(den), emt[:, ch:ch + 1])).astype(BF16)
                ktw = (kt.astype(F32) * w_row[ch:ch + 1, :]).astype(BF16)
                updates.append((idx, decay8[ch:ch + 1, :1] * c_old, ktw, v_aug))

        for idx, c_dec, ktw, v_aug in updates:
            c_scr[idx] = c_dec + _dot(ktw, v_aug)
        return carry

    lax.fori_loop(0, bsz, body, 0, unroll=True)


def _mlstm(q, kt, va, ar, br, cc, bc):
    bsz, seqlen, _ = q.shape
    nc = seqlen // CHUNK
    cps = CHUNKS_PER_STEP
    assert nc % cps == 0 and GROUP_CHUNKS % cps == 0
    ns = nc // cps
    fwd3 = lambda j: (0, j, 0)
    bwd3 = lambda j: (0, ns - 1 - j, 0)
    fwd3t = lambda j: (0, 0, j)
    bwd3t = lambda j: (0, 0, ns - 1 - j)
    tok = (bsz, cps * CHUNK, A_WIDTH)
    tok_t = (bsz, A_WIDTH, cps * CHUNK)
    rowb = (bsz, N_CHAN, cps * CHUNK)
    colb = (bsz, 1, CHUNK, LANES)
    fwd4 = lambda j: (0, (j * cps) // GROUP_CHUNKS, 0, 0)
    bwd4 = lambda j: (0, (nc - 1 - j * cps) // GROUP_CHUNKS, 0, 0)
    in_specs = [
        pl.BlockSpec(tok, fwd3), pl.BlockSpec(tok, bwd3),
        pl.BlockSpec(tok_t, fwd3t), pl.BlockSpec(tok_t, bwd3t),
        pl.BlockSpec(tok, fwd3), pl.BlockSpec(tok, bwd3),
        pl.BlockSpec(rowb, fwd3t), pl.BlockSpec(rowb, bwd3t),
        pl.BlockSpec(rowb, fwd3t), pl.BlockSpec(rowb, bwd3t),
        pl.BlockSpec(colb, fwd4), pl.BlockSpec(colb, bwd4),
        pl.BlockSpec(colb, fwd4), pl.BlockSpec(colb, bwd4),
    ]
    return pl.pallas_call(
        functools.partial(_mlstm_kernel, bsz=bsz),
        grid=(ns,),
        in_specs=in_specs,
        out_specs=(pl.BlockSpec(tok, fwd3), pl.BlockSpec(tok, bwd3)),
        out_shape=(jax.ShapeDtypeStruct((bsz, seqlen, A_WIDTH), BF16),
                   jax.ShapeDtypeStruct((bsz, seqlen, A_WIDTH), BF16)),
        scratch_shapes=[pltpu.VMEM((bsz * 2 * A_HEADS, A_HEAD_DIM, 2 * A_HEAD_DIM), F32),
                        pltpu.VMEM((bsz * 2, N_CHAN, LANES), F32)],
        compiler_params=_params("arbitrary"),
        name="mlstm",
    )(q, q, kt, kt, va, va, ar, ar, br, br, cc, cc, bc, bc)


def _t5_bucket(rel):
    nb = N_BUCKETS // 2
    max_exact = nb // 2
    ret = jnp.where(rel > 0, nb, 0)
    n = jnp.abs(rel)
    nf = jnp.maximum(n, 1).astype(jnp.float32)
    large = max_exact + (jnp.log(nf / max_exact) / math.log(MAX_DISTANCE / max_exact)
                         * (nb - max_exact)).astype(jnp.int32)
    large = jnp.minimum(large, nb - 1)
    return ret + jnp.where(n < max_exact, n, large)


def _bias_kernel(rb_ref, bucket_ref, bias_ref):
    bucket = bucket_ref[...]
    qi = lax.broadcasted_iota(jnp.int32, bucket.shape, 0)
    kj = lax.broadcasted_iota(jnp.int32, bucket.shape, 1)
    band = jnp.abs(kj - BLOCK - qi) <= WINDOW
    masks = (band & (kj >= BLOCK), band, band & (kj < 2 * BLOCK))
    for hq in range(B_Q_HEADS):
        acc = jnp.zeros(bucket.shape, F32)
        for nb in range(N_BUCKETS):
            acc = jnp.where(bucket == nb, rb_ref[nb, hq], acc)
        acc = acc * LOG2E
        for v, mask in enumerate(masks):
            bias_ref[v, hq] = jnp.where(mask, acc, NEG_INF)


def _bias_table(rel_bias):
    q_off = jnp.arange(BLOCK)
    k_off = jnp.arange(3 * BLOCK) - BLOCK
    bucket = _t5_bucket(k_off[None, :] - q_off[:, None]).astype(jnp.int32)
    shape = (3, B_Q_HEADS, BLOCK, 3 * BLOCK)
    return pl.pallas_call(
        _bias_kernel,
        in_specs=[pl.BlockSpec(memory_space=pltpu.SMEM),
                  pl.BlockSpec(bucket.shape, lambda: (0, 0))],
        out_specs=pl.BlockSpec(shape, lambda: (0, 0, 0, 0)),
        out_shape=jax.ShapeDtypeStruct(shape, F32),
        name="bias_table",
    )(rel_bias.astype(F32), bucket)


def _attn_out_kernel(sink_ref, q_ref, kp_ref, kc_ref, kn_ref, vp_ref, vc_ref, vn_ref, zb_ref,
                     bias_ref, hf_ref, hb_ref, og_ref, zg_ref, x_ref, mw_ref, wo_ref, fw_ref,
                     o_ref, s_scr, yb_scr, *, n_qb, layer, final):
    j = pl.program_id(1)
    last = pl.num_programs(1) - 1
    n_keys = (n_qb + 2) * BLOCK
    lo_k = lax.broadcasted_iota(jnp.int32, (n_keys, LANES), 1) < B_HEAD_DIM
    lo_q = lax.broadcasted_iota(jnp.int32, (BLOCK, LANES), 1) < B_HEAD_DIM
    zero = jnp.zeros((n_keys, LANES), BF16)
    k_sel, v_sel = [], []
    for h in range(B_KV_HEADS):
        hs = slice(h * LANES, (h + 1) * LANES)
        k2 = jnp.concatenate([kp_ref[0, :, hs], kc_ref[0, :, hs], kn_ref[0, :, hs]], axis=0)
        v2 = jnp.concatenate([vp_ref[0, :, hs], vc_ref[0, :, hs], vn_ref[0, :, hs]], axis=0)
        k_sel.append((jnp.where(lo_k, k2, zero), jnp.where(lo_k, zero, k2)))
        v_sel.append((jnp.where(lo_k, v2, zero), jnp.where(lo_k, zero, v2)))

    for i in range(n_qb):
        variant = jnp.int32(1)
        if i == 0:
            variant = jnp.where(j == 0, 0, variant)
        if i == n_qb - 1:
            variant = jnp.where(j == last, 2, variant)
        rows = slice(i * BLOCK, (i + 1) * BLOCK)
        win = slice(i * BLOCK, (i + 3) * BLOCK)
        for p_idx in range(B_Q_HEADS // 2):
            qp = q_ref[0, rows, p_idx * LANES:(p_idx + 1) * LANES]
            for par in range(2):
                hq = p_idx * 2 + par
                k_win = k_sel[p_idx // 2][par][win]
                s_scr[i * B_Q_HEADS + hq] = _dot_nt(qp, k_win) + bias_ref[variant, hq]

    for i in range(n_qb):
        rows = slice(i * BLOCK, (i + 1) * BLOCK)
        win = slice(i * BLOCK, (i + 3) * BLOCK)
        for p_idx in range(B_Q_HEADS // 2):
            ps = slice(p_idx * LANES, (p_idx + 1) * LANES)
            probs, dens = [], []
            for par in range(2):
                hq = p_idx * 2 + par
                sink = sink_ref[layer, hq] * LOG2E
                sc = s_scr[i * B_Q_HEADS + hq]
                m = jnp.maximum(jnp.max(sc, axis=-1, keepdims=True), sink)
                p = jnp.exp2(sc - m)
                dens.append(jnp.sum(p, axis=-1, keepdims=True) + jnp.exp2(sink - m))
                probs.append(p.astype(BF16))
            v_even, v_odd = v_sel[p_idx // 2]
            v_bd = jnp.concatenate([v_even[win], v_odd[win]], axis=0)
            out = _dot(jnp.concatenate(probs, axis=1), v_bd)
            y = out / jnp.where(lo_q, dens[0], dens[1])
            yb_scr[rows, ps] = (y * zb_ref[0, rows, ps].astype(F32)).astype(BF16)

    h = og_ref[0].astype(F32) * (hf_ref[0].astype(F32) + hb_ref[0].astype(F32))
    parts = []
    for k in range(A_HEADS):
        hs = slice(k * A_HEAD_DIM, (k + 1) * A_HEAD_DIM)
        hh = h[:, hs]
        hh = hh * lax.rsqrt(jnp.mean(hh * hh, axis=-1, keepdims=True) + EPS)
        parts.append(hh * mw_ref[0, :, hs])
    ya = (jnp.concatenate(parts, axis=1) * zg_ref[0].astype(F32)).astype(BF16)
    out = x_ref[0] + _dot(ya, wo_ref[0, :A_WIDTH, :]) + _dot(yb_scr[...], wo_ref[0, A_WIDTH:, :])
    if final:
        out = out * lax.rsqrt(jnp.mean(out * out, axis=-1, keepdims=True) + EPS) * fw_ref[...]
    o_ref[0] = out


def _attn_out(sink, qb, kb2, vb2, zb, bias, hf, hb, og, zg, x, mw, wo, fw, *, n_qb, layer, final):
    bsz, seqlen, _ = qb.shape
    nb = seqlen // BLOCK
    assert nb >= 2 and nb % n_qb == 0
    tq = n_qb * BLOCK
    cur = lambda b, j: (b, j, 0)
    prev = lambda b, j: (b, jnp.maximum(j * n_qb - 1, 0), 0)
    nxt = lambda b, j: (b, jnp.minimum((j + 1) * n_qb, nb - 1), 0)
    halo = pl.BlockSpec((1, BLOCK, 2 * LANES), prev), pl.BlockSpec((1, BLOCK, 2 * LANES), nxt)
    kv_cur = pl.BlockSpec((1, tq, 2 * LANES), cur)

    def tok(width):
        return pl.BlockSpec((1, tq, width), cur)

    def full(arr):
        return pl.BlockSpec(arr.shape, lambda b, j: (0,) * arr.ndim)

    return pl.pallas_call(
        functools.partial(_attn_out_kernel, n_qb=n_qb, layer=layer, final=final),
        grid=(bsz, nb // n_qb),
        in_specs=[pl.BlockSpec(memory_space=pltpu.SMEM),
                  tok(B_WIDTH),
                  halo[0], kv_cur, halo[1],
                  halo[0], kv_cur, halo[1],
                  tok(B_WIDTH), full(bias),
                  tok(A_WIDTH), tok(A_WIDTH), tok(A_WIDTH), tok(A_WIDTH), tok(D_MODEL),
                  _layer_spec(mw, layer), _layer_spec(wo, layer), full(fw)],
        out_specs=tok(D_MODEL),
        out_shape=jax.ShapeDtypeStruct(x.shape, F32),
        scratch_shapes=[pltpu.VMEM((n_qb * B_Q_HEADS, BLOCK, 3 * BLOCK), F32),
                        pltpu.VMEM((tq, B_WIDTH), BF16)],
        compiler_params=_params("parallel", "parallel"),
        name="attn_out",
    )(sink, qb, kb2, kb2, kb2, vb2, vb2, vb2, zb, bias, hf, hb, og, zg, x, mw, wo, fw)


def _tile(n, target):
    t = min(n, target)
    assert n % t == 0, (n, t)
    return t


def kernel(x, norm_w, w_in, conv_w, conv_b, gate_b, mhn_w, sink, rel_bias, w_out, final_norm_w):
    bsz, seqlen, d_model = x.shape
    depth = norm_w.shape[0]
    assert d_model == D_MODEL and seqlen % CHUNK == 0
    tm = _tile(seqlen, 1024)
    n_qb = _tile(seqlen // BLOCK, 4)

    c_qk = 2 * A_WIDTH
    c_a = c_qk + 3 * A_WIDTH
    c_g = c_a + N_GATES
    w_bf = w_in.astype(BF16)
    assert c_a == 5 * A_WIDTH
    wgt = jnp.swapaxes(w_bf[:, :, c_a:c_g], 1, 2)
    wb = w_bf[:, :, c_g:]
    wo = w_out.astype(BF16)
    cw = jnp.pad(conv_w, ((0, 0), (0, SUBLANES - CONV_K), (0, 0)))
    nw = norm_w.reshape(depth, 1, D_MODEL)
    cb = conv_b.reshape(depth, 1, 2 * A_WIDTH)
    gb = gate_b.reshape(depth, N_GATES, 1)
    mw = mhn_w.reshape(depth, 1, A_WIDTH)
    bias = _bias_table(rel_bias)
    fw = final_norm_w.reshape(1, D_MODEL)

    xf = x
    for l in range(depth):
        q, kt, va, og, zg, qb, kb2, vb2, zb, gr = _in_proj(
            xf, nw, w_bf, wgt, wb, cw, cb, gb, tm=tm, layer=l)
        ar, br, cc, bc = _gate_prep(gr)
        hf, hb = _mlstm(q, kt, va, ar, br, cc, bc)
        xf = _attn_out(sink, qb, kb2, vb2, zb, bias, hf, hb, og, zg, xf, mw, wo, fw,
                       n_qb=n_qb, layer=l, final=(l == depth - 1))
    return xf
```

```python
import functools
import math

import jax
import jax.numpy as jnp
import numpy as np
from jax import lax
from jax.experimental import pallas as pl
from jax.experimental.pallas import tpu as pltpu

D_MODEL = 1024
A_WIDTH = 512
A_HEADS = 4
A_HEAD_DIM = 128
CHUNK = 128
CONV_K = 5
B_WIDTH = 512
B_HEAD_DIM = 64
B_Q_HEADS = 8
B_KV_HEADS = 2
WINDOW = 128
BLOCK = 128
N_BUCKETS = 32
MAX_DISTANCE = 128
EPS = 1e-6
NEG_INF = -1e30
LOG2E = math.log2(math.e)
N_GATES = 4 * A_HEADS
N_CHAN = 2 * A_HEADS

LANES = 128
SUBLANES = 8
VMEM_LIMIT_BYTES = 56 * 1024 * 1024

HALO = 2 * SUBLANES
BF16 = jnp.bfloat16
F32 = jnp.float32


def _params(*sem):
    return pltpu.CompilerParams(dimension_semantics=sem, vmem_limit_bytes=VMEM_LIMIT_BYTES)


def _dot(a, b):
    return jnp.dot(a, b, preferred_element_type=F32)


def _dot_nt(a, b):
    return lax.dot_general(a, b, (((1,), (1,)), ((), ())), preferred_element_type=F32)


def _log_sigmoid(x):
    return -(jnp.maximum(-x, 0.0) + jnp.log1p(jnp.exp(-jnp.abs(x))))


def _sigmoid(x):
    return 0.5 * jnp.tanh(0.5 * x) + 0.5


def _silu_of_half(h):
    return h + h * jnp.tanh(h)


def _layer_spec(arr, layer):
    return pl.BlockSpec((1,) + arr.shape[1:], lambda *_: (layer,) + (0,) * (arr.ndim - 1))


def _in_proj_kernel(x_ref, xp_ref, xn_ref, nw_ref, wqk_ref, wv_ref, wo_ref, wz_ref, wgt_ref, wb_ref,
                    cw_ref, cb_ref, gb_ref,
                    q_ref, kt_ref, va_ref, og_ref, zg_ref, qb_ref, kb2_ref, vb2_ref, zb_ref, gr_ref,
                    u_scr, *, tm):
    i = pl.program_id(1)
    last = pl.num_programs(1) - 1
    nw = nw_ref[0]

    def norm(xv):
        y = xv * lax.rsqrt(jnp.mean(xv * xv, axis=-1, keepdims=True) + EPS)
        return (y * nw).astype(BF16)

    hn = norm(x_ref[0])
    hp = norm(xp_ref[0])
    hx = norm(xn_ref[0])
    u_all = _dot(jnp.concatenate([hp, hn, hx], axis=0), wqk_ref[0])
    u_top = jnp.where(i == 0, 0.0, u_all[:HALO])
    u_bot = jnp.where(i == last, 0.0, u_all[HALO + tm:])
    cw_half = 0.5 * cw_ref[0]
    cb_half = 0.5 * cb_ref[0]
    parts = []
    pad = CONV_K // 2
    for c in range(2 * A_WIDTH // LANES):
        cs = slice(c * LANES, (c + 1) * LANES)
        u_scr[c, :HALO, :] = u_top[:, cs]
        u_scr[c, HALO:HALO + tm, :] = u_all[HALO:HALO + tm, cs]
        u_scr[c, HALO + tm:, :] = u_bot[:, cs]
        acc = cb_half[:, cs]
        for tap in range(CONV_K):
            lo = HALO - pad + tap
            acc = acc + u_scr[c, lo:lo + tm, :] * cw_half[tap:tap + 1, cs]
        parts.append(acc)
    qk = _silu_of_half(jnp.concatenate(parts, axis=1))
    q_ref[0] = qk[:, :A_WIDTH].astype(BF16)
    k = qk[:, A_WIDTH:] * (A_HEAD_DIM ** -0.5)
    kt_ref[0] = k.T.astype(BF16)

    va_ref[0] = _dot(hn, wv_ref[0]).astype(BF16)
    og_ref[0] = _sigmoid(_dot(hn, wo_ref[0])).astype(BF16)
    zg_ref[0] = _silu_of_half(0.5 * _dot(hn, wz_ref[0])).astype(BF16)

    bq = _dot(hn, wb_ref[0, :, :B_WIDTH])
    qb_ref[0] = (bq * (B_HEAD_DIM ** -0.5 * LOG2E)).astype(BF16)
    kv = _dot(hn, wb_ref[0, :, B_WIDTH:B_WIDTH + 2 * LANES])
    half = lax.broadcasted_iota(jnp.int32, (tm, LANES), 1) < B_HEAD_DIM
    for src, dst in ((kv[:, :LANES], kb2_ref), (kv[:, LANES:], vb2_ref)):
        sw = pltpu.roll(src, B_HEAD_DIM, 1)
        dst[0, :, :LANES] = jnp.where(half, src, sw).astype(BF16)
        dst[0, :, LANES:] = jnp.where(half, sw, src).astype(BF16)
    zb_ref[0] = _silu_of_half(0.5 * _dot(hn, wb_ref[0, :, B_WIDTH + 2 * LANES:])).astype(BF16)

    g = _dot_nt(wgt_ref[0], hn) + gb_ref[0]
    row = lax.broadcasted_iota(jnp.int32, g.shape, 0)
    gr_ref[0] = jnp.where(row < N_CHAN, g, _log_sigmoid(g))


def _in_proj(x, nw, w_all, wgt, wb, cw, cb, gb, *, tm, layer):
    bsz, seqlen, _ = x.shape
    nt = seqlen // tm
    hb = tm // HALO
    nhb = seqlen // HALO

    def full(arr):
        return _layer_spec(arr, layer)

    def w_cols(start, width):
        assert start % width == 0 and width % LANES == 0
        return pl.BlockSpec((1, D_MODEL, width), lambda b, i: (layer, 0, start // width))

    def rows(width):
        return pl.BlockSpec((1, tm, width), lambda b, i: (b, i, 0))

    out_shape = (
        jax.ShapeDtypeStruct((bsz, seqlen, A_WIDTH), BF16),
        jax.ShapeDtypeStruct((bsz, A_WIDTH, seqlen), BF16),
        jax.ShapeDtypeStruct((bsz, seqlen, A_WIDTH), BF16),
        jax.ShapeDtypeStruct((bsz, seqlen, A_WIDTH), BF16),
        jax.ShapeDtypeStruct((bsz, seqlen, A_WIDTH), BF16),
        jax.ShapeDtypeStruct((bsz, seqlen, B_WIDTH), BF16),
        jax.ShapeDtypeStruct((bsz, seqlen, 2 * LANES), BF16),
        jax.ShapeDtypeStruct((bsz, seqlen, 2 * LANES), BF16),
        jax.ShapeDtypeStruct((bsz, seqlen, B_WIDTH), BF16),
        jax.ShapeDtypeStruct((bsz, N_GATES, seqlen), F32),
    )
    out_specs = (
        rows(A_WIDTH),
        pl.BlockSpec((1, A_WIDTH, tm), lambda b, i: (b, 0, i)),
        rows(A_WIDTH), rows(A_WIDTH), rows(A_WIDTH), rows(B_WIDTH),
        rows(2 * LANES), rows(2 * LANES), rows(B_WIDTH),
        pl.BlockSpec((1, N_GATES, tm), lambda b, i: (b, 0, i)),
    )
    in_specs = [
        rows(D_MODEL),
        pl.BlockSpec((1, HALO, D_MODEL), lambda b, i: (b, jnp.maximum(i * hb - 1, 0), 0)),
        pl.BlockSpec((1, HALO, D_MODEL), lambda b, i: (b, jnp.minimum((i + 1) * hb, nhb - 1), 0)),
        full(nw),
        w_cols(0, 2 * A_WIDTH),
        w_cols(2 * A_WIDTH, A_WIDTH),
        w_cols(3 * A_WIDTH, A_WIDTH),
        w_cols(4 * A_WIDTH, A_WIDTH),
        full(wgt), full(wb), full(cw), full(cb), full(gb),
    ]
    return pl.pallas_call(
        functools.partial(_in_proj_kernel, tm=tm),
        grid=(bsz, nt),
        in_specs=in_specs,
        out_specs=out_specs,
        out_shape=out_shape,
        scratch_shapes=[pltpu.VMEM((2 * A_WIDTH // LANES, tm + 2 * HALO, LANES), F32)],
        compiler_params=_params("parallel", "arbitrary"),
        name="in_proj",
    )(x, x, x, nw, w_all, w_all, w_all, w_all, wgt, wb, cw, cb, gb)


GROUP_CHUNKS = LANES // N_CHAN


def _gate_prep_kernel(gr_ref, ar_ref, br_ref, cpk_ref, bpk_ref):
    ti = lax.broadcasted_iota(jnp.int32, (CHUNK, CHUNK), 0)
    si = lax.broadcasted_iota(jnp.int32, (CHUNK, CHUNK), 1)
    upper_f = (ti <= si).astype(F32)
    lower_f = (ti >= si).astype(F32)
    fwd_row = lax.rem(ti, N_CHAN) < A_HEADS
    li = jnp.concatenate([gr_ref[0, :N_CHAN, c * CHUNK:(c + 1) * CHUNK]
                          for c in range(GROUP_CHUNKS)], axis=0) * LOG2E
    lf = jnp.concatenate([gr_ref[0, N_CHAN:, c * CHUNK:(c + 1) * CHUNK]
                          for c in range(GROUP_CHUNKS)], axis=0) * LOG2E
    b_pre = jnp.dot(lf, upper_f, preferred_element_type=F32, precision=lax.Precision.HIGHEST)
    b_suf = jnp.dot(lf, lower_f, preferred_element_type=F32, precision=lax.Precision.HIGHEST)
    b = jnp.where(fwd_row, b_pre, b_suf)
    a = li - b
    pre, suf = a, a
    sh = 1
    while sh < CHUNK:
        pre = jnp.where(si >= sh, jnp.maximum(pre, pltpu.roll(pre, sh, 1)), pre)
        suf = jnp.where(si < CHUNK - sh, jnp.maximum(suf, pltpu.roll(suf, CHUNK - sh, 1)), suf)
        sh *= 2
    cmax = jnp.where(fwd_row, pre, suf)
    for c in range(GROUP_CHUNKS):
        rs = slice(c * N_CHAN, (c + 1) * N_CHAN)
        ar_ref[0, :, c * CHUNK:(c + 1) * CHUNK] = a[rs, :]
        br_ref[0, :, c * CHUNK:(c + 1) * CHUNK] = b[rs, :]
    cpk_ref[0, 0] = cmax.T
    bpk_ref[0, 0] = b.T


def _gate_prep(gr):
    bsz, _, seqlen = gr.shape
    tg = GROUP_CHUNKS * CHUNK
    assert seqlen % tg == 0
    row_spec = pl.BlockSpec((1, N_CHAN, tg), lambda b, i: (b, 0, i))
    pk_spec = pl.BlockSpec((1, 1, CHUNK, LANES), lambda b, i: (b, i, 0, 0))
    pk_shape = jax.ShapeDtypeStruct((bsz, seqlen // tg, CHUNK, LANES), F32)
    return pl.pallas_call(
        _gate_prep_kernel,
        grid=(bsz, seqlen // tg),
        in_specs=[pl.BlockSpec((1, N_GATES, tg), lambda b, i: (b, 0, i))],
        out_specs=(row_spec, row_spec, pk_spec, pk_spec),
        out_shape=(jax.ShapeDtypeStruct((bsz, N_CHAN, seqlen), F32),
                   jax.ShapeDtypeStruct((bsz, N_CHAN, seqlen), F32),
                   pk_shape, pk_shape),
        compiler_params=_params("parallel", "parallel"),
        name="gate_prep",
    )(gr)


CHUNKS_PER_STEP = 4


def _mlstm_kernel(qf_ref, qb_ref, ktf_ref, ktb_ref, vf_ref, vb_ref,
                  arf_ref, arb_ref, brf_ref, brb_ref, ccf_ref, ccb_ref, bcf_ref, bcb_ref,
                  hf_ref, hb_ref, c_scr, m_scr, *, bsz):
    j = pl.program_id(0)

    @pl.when(j == 0)
    def _():
        c_scr[...] = jnp.zeros_like(c_scr)
        m_scr[...] = jnp.zeros_like(m_scr)

    ti = lax.broadcasted_iota(jnp.int32, (CHUNK, CHUNK), 0)
    si = lax.broadcasted_iota(jnp.int32, (CHUNK, CHUNK), 1)
    ones_blk = jnp.ones((CHUNK, A_HEAD_DIM), BF16)
    sub8 = lax.broadcasted_iota(jnp.int32, (N_CHAN, LANES), 0)
    lane8 = lax.broadcasted_iota(jnp.int32, (N_CHAN, LANES), 1)

    n_steps = pl.num_programs(0)
    for k in range(CHUNKS_PER_STEP):
        _mlstm_chunk(j, k, n_steps, qf_ref, qb_ref, ktf_ref, ktb_ref, vf_ref, vb_ref,
                     arf_ref, arb_ref, brf_ref, brb_ref, ccf_ref, ccb_ref, bcf_ref, bcb_ref,
                     hf_ref, hb_ref, c_scr, m_scr, bsz, ti, si, ones_blk, sub8, lane8)


def _mlstm_chunk(j, k, n_steps, qf_ref, qb_ref, ktf_ref, ktb_ref, vf_ref, vb_ref,
                 arf_ref, arb_ref, brf_ref, brb_ref, ccf_ref, ccb_ref, bcf_ref, bcb_ref,
                 hf_ref, hb_ref, c_scr, m_scr, bsz, ti, si, ones_blk, sub8, lane8):
    jf = j * CHUNKS_PER_STEP + k
    jb = n_steps * CHUNKS_PER_STEP - 1 - jf
    base_f = lax.rem(jf, GROUP_CHUNKS) * N_CHAN
    base_b = lax.rem(jb, GROUP_CHUNKS) * N_CHAN
    rows_f = slice(k * CHUNK, (k + 1) * CHUNK)
    rows_b = slice((CHUNKS_PER_STEP - 1 - k) * CHUNK, (CHUNKS_PER_STEP - k) * CHUNK)
    dirs = (
        (0, qf_ref, ktf_ref, vf_ref, arf_ref, brf_ref, ccf_ref, bcf_ref, hf_ref, ti >= si, CHUNK - 1,
         base_f, rows_f),
        (1, qb_ref, ktb_ref, vb_ref, arb_ref, brb_ref, ccb_ref, bcb_ref, hb_ref, ti <= si, 0,
         base_b, rows_b),
    )

    def body(b, carry):
        tiles = []
        for d, q_ref, kt_ref, v_ref, ar_ref, br_ref, cc_ref, bc_ref, h_ref, vis, last, base, rows in dirs:
            for h in range(A_HEADS):
                hs = slice(h * A_HEAD_DIM, (h + 1) * A_HEAD_DIM)
                idx = (b * 2 + d) * A_HEADS + h
                q = q_ref[b, rows, hs]
                kt = kt_ref[b, hs, rows]
                tiles.append((_dot(q, kt), q, kt, idx, hs))

        gates = []
        for d, q_ref, kt_ref, v_ref, ar_ref, br_ref, cc_ref, bc_ref, h_ref, vis, last, base, rows in dirs:
            a_row = ar_ref[b, :, rows]
            b_last = br_ref[b, :, rows][:, last:last + 1]
            m8 = m_scr[b * 2 + d]
            gl8 = jnp.maximum(m8, jnp.max(a_row, axis=1, keepdims=True))
            w_row = jnp.exp2(a_row - gl8)
            decay8 = jnp.exp2(m8 - gl8)
            m_scr[b * 2 + d] = b_last + gl8
            m_lane = jnp.sum(jnp.where(sub8 == lane8, m8, 0.0), axis=0, keepdims=True)
            unrot = lax.rem(LANES - base, LANES)
            g = jnp.maximum(m_lane, pltpu.roll(cc_ref[b, 0], unrot, 1))
            emt = jnp.exp2(-(pltpu.roll(bc_ref[b, 0], unrot, 1) + g))
            gates.append((a_row, w_row, decay8, m8, g, emt))

        updates = []
        for d, q_ref, kt_ref, v_ref, ar_ref, br_ref, cc_ref, bc_ref, h_ref, vis, last, base, rows in dirs:
            a_row, w_row, decay8, m8, g, emt = gates[d]
            for h in range(A_HEADS):
                ch = d * A_HEADS + h
                qk, q, kt, idx, hs = tiles[ch]
                c_old = c_scr[idx]
                v_aug = jnp.concatenate([v_ref[b, rows, hs], ones_blk], axis=1)
                g_b = jnp.broadcast_to(g[:, ch:ch + 1], (CHUNK, LANES))
                p = jnp.where(vis, jnp.exp2(a_row[ch:ch + 1, :] - g_b), 0.0)
                iw_b = jnp.exp2(m8[ch:ch + 1, :] - g_b)
                lhs = jnp.concatenate([(qk * p).astype(BF16),
                                       (q.astype(F32) * iw_b).astype(BF16)], axis=1)
                rhs = jnp.concatenate([v_aug, c_old.astype(BF16)], axis=0)
                r = _dot(lhs, rhs)
                num = r[:, :A_HEAD_DIM]
                den = r[:, A_HEAD_DIM:]
                h_ref[b, rows, hs] = (num / jnp.maximum(jnp.abs(den), emt[:, ch:ch + 1])).astype(BF16)
                ktw = (kt.astype(F32) * w_row[ch:ch + 1, :]).astype(BF16)
                updates.append((idx, decay8[ch:ch + 1, :1] * c_old, ktw, v_aug))

        for idx, c_dec, ktw, v_aug in updates:
            c_scr[idx] = c_dec + _dot(ktw, v_aug)
        return carry

    lax.fori_loop(0, bsz, body, 0, unroll=True)


def _mlstm(q, kt, va, ar, br, cc, bc):
    bsz, seqlen, _ = q.shape
    nc = seqlen // CHUNK
    cps = CHUNKS_PER_STEP
    assert nc % cps == 0 and GROUP_CHUNKS % cps == 0
    ns = nc // cps
    fwd3 = lambda j: (0, j, 0)
    bwd3 = lambda j: (0, ns - 1 - j, 0)
    fwd3t = lambda j: (0, 0, j)
    bwd3t = lambda j: (0, 0, ns - 1 - j)
    tok = (bsz, cps * CHUNK, A_WIDTH)
    tok_t = (bsz, A_WIDTH, cps * CHUNK)
    rowb = (bsz, N_CHAN, cps * CHUNK)
    colb = (bsz, 1, CHUNK, LANES)
    fwd4 = lambda j: (0, (j * cps) // GROUP_CHUNKS, 0, 0)
    bwd4 = lambda j: (0, (nc - 1 - j * cps) // GROUP_CHUNKS, 0, 0)
    in_specs = [
        pl.BlockSpec(tok, fwd3), pl.BlockSpec(tok, bwd3),
        pl.BlockSpec(tok_t, fwd3t), pl.BlockSpec(tok_t, bwd3t),
        pl.BlockSpec(tok, fwd3), pl.BlockSpec(tok, bwd3),
        pl.BlockSpec(rowb, fwd3t), pl.BlockSpec(rowb, bwd3t),
        pl.BlockSpec(rowb, fwd3t), pl.BlockSpec(rowb, bwd3t),
        pl.BlockSpec(colb, fwd4), pl.BlockSpec(colb, bwd4),
        pl.BlockSpec(colb, fwd4), pl.BlockSpec(colb, bwd4),
    ]
    return pl.pallas_call(
        functools.partial(_mlstm_kernel, bsz=bsz),
        grid=(ns,),
        in_specs=in_specs,
        out_specs=(pl.BlockSpec(tok, fwd3), pl.BlockSpec(tok, bwd3)),
        out_shape=(jax.ShapeDtypeStruct((bsz, seqlen, A_WIDTH), BF16),
                   jax.ShapeDtypeStruct((bsz, seqlen, A_WIDTH), BF16)),
        scratch_shapes=[pltpu.VMEM((bsz * 2 * A_HEADS, A_HEAD_DIM, 2 * A_HEAD_DIM), F32),
                        pltpu.VMEM((bsz * 2, N_CHAN, LANES), F32)],
        compiler_params=_params("arbitrary"),
        name="mlstm",
    )(q, q, kt, kt, va, va, ar, ar, br, br, cc, cc, bc, bc)


def _t5_bucket(rel):
    nb = N_BUCKETS // 2
    max_exact = nb // 2
    ret = jnp.where(rel > 0, nb, 0)
    n = jnp.abs(rel)
    nf = jnp.maximum(n, 1).astype(jnp.float32)
    large = max_exact + (jnp.log(nf / max_exact) / math.log(MAX_DISTANCE / max_exact)
                         * (nb - max_exact)).astype(jnp.int32)
    large = jnp.minimum(large, nb - 1)
    return ret + jnp.where(n < max_exact, n, large)


def _bias_kernel(rb_ref, bucket_ref, bias_ref):
    bucket = bucket_ref[...]
    qi = lax.broadcasted_iota(jnp.int32, bucket.shape, 0)
    kj = lax.broadcasted_iota(jnp.int32, bucket.shape, 1)
    band = jnp.abs(kj - BLOCK - qi) <= WINDOW
    masks = (band & (kj >= BLOCK), band, band & (kj < 2 * BLOCK))
    for hq in range(B_Q_HEADS):
        acc = jnp.zeros(bucket.shape, F32)
        for nb in range(N_BUCKETS):
            acc = jnp.where(bucket == nb, rb_ref[nb, hq], acc)
        acc = acc * LOG2E
        for v, mask in enumerate(masks):
            bias_ref[v, hq] = jnp.where(mask, acc, NEG_INF)


def _bias_table(rel_bias):
    q_off = jnp.arange(BLOCK)
    k_off = jnp.arange(3 * BLOCK) - BLOCK
    bucket = _t5_bucket(k_off[None, :] - q_off[:, None]).astype(jnp.int32)
    shape = (3, B_Q_HEADS, BLOCK, 3 * BLOCK)
    return pl.pallas_call(
        _bias_kernel,
        in_specs=[pl.BlockSpec(memory_space=pltpu.SMEM),
                  pl.BlockSpec(bucket.shape, lambda: (0, 0))],
        out_specs=pl.BlockSpec(shape, lambda: (0, 0, 0, 0)),
        out_shape=jax.ShapeDtypeStruct(shape, F32),
        name="bias_table",
    )(rel_bias.astype(F32), bucket)


def _attn_out_kernel(sink_ref, q_ref, kp_ref, kc_ref, kn_ref, vp_ref, vc_ref, vn_ref, zb_ref,
                     bias_ref, hf_ref, hb_ref, og_ref, zg_ref, x_ref, mw_ref, wo_ref, fw_ref,
                     o_ref, s_scr, yb_scr, *, n_qb, layer, final):
    j = pl.program_id(1)
    last = pl.num_programs(1) - 1
    n_keys = (n_qb + 2) * BLOCK
    lo_k = lax.broadcasted_iota(jnp.int32, (n_keys, LANES), 1) < B_HEAD_DIM
    lo_q = lax.broadcasted_iota(jnp.int32, (BLOCK, LANES), 1) < B_HEAD_DIM
    zero = jnp.zeros((n_keys, LANES), BF16)
    k_sel, v_sel = [], []
    for h in range(B_KV_HEADS):
        hs = slice(h * LANES, (h + 1) * LANES)
        k2 = jnp.concatenate([kp_ref[0, :, hs], kc_ref[0, :, hs], kn_ref[0, :, hs]], axis=0)
        v2 = jnp.concatenate([vp_ref[0, :, hs], vc_ref[0, :, hs], vn_ref[0, :, hs]], axis=0)
        k_sel.append((jnp.where(lo_k, k2, zero), jnp.where(lo_k, zero, k2)))
        v_sel.append((jnp.where(lo_k, v2, zero), jnp.where(lo_k, zero, v2)))

    for i in range(n_qb):
        variant = jnp.int32(1)
        if i == 0:
            variant = jnp.where(j == 0, 0, variant)
        if i == n_qb - 1:
            variant = jnp.where(j == last, 2, variant)
        rows = slice(i * BLOCK, (i + 1) * BLOCK)
        win = slice(i * BLOCK, (i + 3) * BLOCK)
        for p_idx in range(B_Q_HEADS // 2):
            qp = q_ref[0, rows, p_idx * LANES:(p_idx + 1) * LANES]
            for par in range(2):
                hq = p_idx * 2 + par
                k_win = k_sel[p_idx // 2][par][win]
                s_scr[i * B_Q_HEADS + hq] = _dot_nt(qp, k_win) + bias_ref[variant, hq]

    for i in range(n_qb):
        rows = slice(i * BLOCK, (i + 1) * BLOCK)
        win = slice(i * BLOCK, (i + 3) * BLOCK)
        for p_idx in range(B_Q_HEADS // 2):
            ps = slice(p_idx * LANES, (p_idx + 1) * LANES)
            probs, dens = [], []
            for par in range(2):
                hq = p_idx * 2 + par
                sink = sink_ref[layer, hq] * LOG2E
                sc = s_scr[i * B_Q_HEADS + hq]
                m = jnp.maximum(jnp.max(sc, axis=-1, keepdims=True), sink)
                p = jnp.exp2(sc - m)
                dens.append(jnp.sum(p, axis=-1, keepdims=True) + jnp.exp2(sink - m))
                probs.append(p.astype(BF16))
            v_even, v_odd = v_sel[p_idx // 2]
            v_bd = jnp.concatenate([v_even[win], v_odd[win]], axis=0)
            out = _dot(jnp.concatenate(probs, axis=1), v_bd)
            y = out / jnp.where(lo_q, dens[0], dens[1])
            yb_scr[rows, ps] = (y * zb_ref[0, rows, ps].astype(F32)).astype(BF16)

    h = og_ref[0].astype(F32) * (hf_ref[0].astype(F32) + hb_ref[0].astype(F32))
    parts = []
    for k in range(A_HEADS):
        hs = slice(k * A_HEAD_DIM, (k + 1) * A_HEAD_DIM)
        hh = h[:, hs]
        hh = hh * lax.rsqrt(jnp.mean(hh * hh, axis=-1, keepdims=True) + EPS)
        parts.append(hh * mw_ref[0, :, hs])
    ya = (jnp.concatenate(parts, axis=1) * zg_ref[0].astype(F32)).astype(BF16)
    out = x_ref[0] + _dot(ya, wo_ref[0, :A_WIDTH, :]) + _dot(yb_scr[...], wo_ref[0, A_WIDTH:, :])
    if final:
        out = out * lax.rsqrt(jnp.mean(out * out, axis=-1, keepdims=True) + EPS) * fw_ref[...]
    o_ref[0] = out


def _attn_out(sink, qb, kb2, vb2, zb, bias, hf, hb, og, zg, x, mw, wo, fw, *, n_qb, layer, final):
    bsz, seqlen, _ = qb.shape
    nb = seqlen // BLOCK
    assert nb >= 2 and nb % n_qb == 0
    tq = n_qb * BLOCK
    cur = lambda b, j: (b, j, 0)
    prev = lambda b, j: (b, jnp.maximum(j * n_qb - 1, 0), 0)
    nxt = lambda b, j: (b, jnp.minimum((j + 1) * n_qb, nb - 1), 0)
    halo = pl.BlockSpec((1, BLOCK, 2 * LANES), prev), pl.BlockSpec((1, BLOCK, 2 * LANES), nxt)
    kv_cur = pl.BlockSpec((1, tq, 2 * LANES), cur)

    def tok(width):
        return pl.BlockSpec((1, tq, width), cur)

    def full(arr):
        return pl.BlockSpec(arr.shape, lambda b, j: (0,) * arr.ndim)

    return pl.pallas_call(
        functools.partial(_attn_out_kernel, n_qb=n_qb, layer=layer, final=final),
        grid=(bsz, nb // n_qb),
        in_specs=[pl.BlockSpec(memory_space=pltpu.SMEM),
                  tok(B_WIDTH),
                  halo[0], kv_cur, halo[1],
                  halo[0], kv_cur, halo[1],
                  tok(B_WIDTH), full(bias),
                  tok(A_WIDTH), tok(A_WIDTH), tok(A_WIDTH), tok(A_WIDTH), tok(D_MODEL),
                  _layer_spec(mw, layer), _layer_spec(wo, layer), full(fw)],
        out_specs=tok(D_MODEL),
        out_shape=jax.ShapeDtypeStruct(x.shape, F32),
        scratch_shapes=[pltpu.VMEM((n_qb * B_Q_HEADS, BLOCK, 3 * BLOCK), F32),
                        pltpu.VMEM((tq, B_WIDTH), BF16)],
        compiler_params=_params("parallel", "parallel"),
        name="attn_out",
    )(sink, qb, kb2, kb2, kb2, vb2, vb2, vb2, zb, bias, hf, hb, og, zg, x, mw, wo, fw)


def _weight_prep_kernel(w_ref, wbf_ref, wb_ref, *, b_start):
    w = w_ref[0]
    wbf_ref[0] = w.astype(BF16)
    wb_ref[0] = w[:, b_start:].astype(BF16)


def _weight_prep(w_in, b_start, *, rows):
    depth, d_model, cols = w_in.shape
    spec = lambda width: pl.BlockSpec((1, rows, width), lambda l, i: (l, i, 0))
    return pl.pallas_call(
        functools.partial(_weight_prep_kernel, b_start=b_start),
        grid=(depth, d_model // rows),
        in_specs=[spec(cols)],
        out_specs=(spec(cols), spec(cols - b_start)),
        out_shape=(jax.ShapeDtypeStruct(w_in.shape, BF16),
                   jax.ShapeDtypeStruct((depth, d_model, cols - b_start), BF16)),
        compiler_params=_params("parallel", "parallel"),
        name="weight_prep",
    )(w_in)


def _cast_kernel(w_ref, o_ref):
    o_ref[...] = w_ref[...].astype(BF16)


def _cast_bf16(w, *, rows):
    depth, r, c = w.shape
    spec = pl.BlockSpec((1, rows, c), lambda l, i: (l, i, 0))
    return pl.pallas_call(
        _cast_kernel,
        grid=(depth, r // rows),
        in_specs=[spec],
        out_specs=spec,
        out_shape=jax.ShapeDtypeStruct(w.shape, BF16),
        compiler_params=_params("parallel", "parallel"),
        name="cast_bf16",
    )(w)


def _tile(n, target):
    t = min(n, target)
    assert n % t == 0, (n, t)
    return t


def kernel(x, norm_w, w_in, conv_w, conv_b, gate_b, mhn_w, sink, rel_bias, w_out, final_norm_w):
    bsz, seqlen, d_model = x.shape
    depth = norm_w.shape[0]
    assert d_model == D_MODEL and seqlen % CHUNK == 0
    tm = _tile(seqlen, 1024)
    n_qb = _tile(seqlen // BLOCK, 4)

    c_qk = 2 * A_WIDTH
    c_a = c_qk + 3 * A_WIDTH
    c_g = c_a + N_GATES
    assert c_a == 5 * A_WIDTH
    w_bf, wb = _weight_prep(w_in, c_g, rows=_tile(D_MODEL, 256))
    wgt = jnp.swapaxes(w_bf[:, :, c_a:c_g], 1, 2)
    wo = _cast_bf16(w_out, rows=_tile(D_MODEL, 512))
    cw = jnp.pad(conv_w, ((0, 0), (0, SUBLANES - CONV_K), (0, 0)))
    nw = norm_w.reshape(depth, 1, D_MODEL)
    cb = conv_b.reshape(depth, 1, 2 * A_WIDTH)
    gb = gate_b.reshape(depth, N_GATES, 1)
    mw = mhn_w.reshape(depth, 1, A_WIDTH)
    bias = _bias_table(rel_bias)
    fw = final_norm_w.reshape(1, D_MODEL)

    xf = x
    for l in range(depth):
        q, kt, va, og, zg, qb, kb2, vb2, zb, gr = _in_proj(
            xf, nw, w_bf, wgt, wb, cw, cb, gb, tm=tm, layer=l)
        ar, br, cc, bc = _gate_prep(gr)
        hf, hb = _mlstm(q, kt, va, ar, br, cc, bc)
        xf = _attn_out(sink, qb, kb2, vb2, zb, bias, hf, hb, og, zg, xf, mw, wo, fw,
                       n_qb=n_qb, layer=l, final=(l == depth - 1))
    return xf
```

```python
import functools
import math

import jax
import jax.numpy as jnp
import numpy as np
from jax import lax
from jax.experimental import pallas as pl
from jax.experimental.pallas import tpu as pltpu

D_MODEL = 1024
A_WIDTH = 512
A_HEADS = 4
A_HEAD_DIM = 128
CHUNK = 128
CONV_K = 5
B_WIDTH = 512
B_HEAD_DIM = 64
B_Q_HEADS = 8
B_KV_HEADS = 2
WINDOW = 128
BLOCK = 128
N_BUCKETS = 32
MAX_DISTANCE = 128
EPS = 1e-6
NEG_INF = -1e30
LOG2E = math.log2(math.e)
N_GATES = 4 * A_HEADS
N_CHAN = 2 * A_HEADS

LANES = 128
SUBLANES = 8
VMEM_LIMIT_BYTES = 56 * 1024 * 1024

HALO = 2 * SUBLANES
BF16 = jnp.bfloat16
F32 = jnp.float32


def _params(*sem):
    return pltpu.CompilerParams(dimension_semantics=sem, vmem_limit_bytes=VMEM_LIMIT_BYTES)


def _dot(a, b):
    return jnp.dot(a, b, preferred_element_type=F32)


def _dot_nt(a, b):
    return lax.dot_general(a, b, (((1,), (1,)), ((), ())), preferred_element_type=F32)


def _log_sigmoid(x):
    return -(jnp.maximum(-x, 0.0) + jnp.log1p(jnp.exp(-jnp.abs(x))))


def _sigmoid(x):
    return 0.5 * jnp.tanh(0.5 * x) + 0.5


def _silu_of_half(h):
    return h + h * jnp.tanh(h)


def _layer_spec(arr, layer):
    return pl.BlockSpec((1,) + arr.shape[1:], lambda *_: (layer,) + (0,) * (arr.ndim - 1))


def _in_proj_kernel(x_ref, xp_ref, xn_ref, nw_ref, wqk_ref, wv_ref, wo_ref, wz_ref, wgt_ref, wb_ref,
                    cw_ref, cb_ref, gb_ref,
                    q_ref, kt_ref, va_ref, og_ref, zg_ref, qb_ref, kb2_ref, vb2_ref, zb_ref, gr_ref,
                    u_scr, *, tm):
    i = pl.program_id(1)
    last = pl.num_programs(1) - 1
    nw = nw_ref[0]

    def norm(xv):
        y = xv * lax.rsqrt(jnp.mean(xv * xv, axis=-1, keepdims=True) + EPS)
        return (y * nw).astype(BF16)

    hn = norm(x_ref[0])
    hp = norm(xp_ref[0])
    hx = norm(xn_ref[0])
    u_all = _dot(jnp.concatenate([hp, hn, hx], axis=0), wqk_ref[0])
    u_top = jnp.where(i == 0, 0.0, u_all[:HALO])
    u_bot = jnp.where(i == last, 0.0, u_all[HALO + tm:])
    cw_half = 0.5 * cw_ref[0]
    cb_half = 0.5 * cb_ref[0]
    parts = []
    pad = CONV_K // 2
    for c in range(2 * A_WIDTH // LANES):
        cs = slice(c * LANES, (c + 1) * LANES)
        u_scr[c, :HALO, :] = u_top[:, cs]
        u_scr[c, HALO:HALO + tm, :] = u_all[HALO:HALO + tm, cs]
        u_scr[c, HALO + tm:, :] = u_bot[:, cs]
        acc = cb_half[:, cs]
        for tap in range(CONV_K):
            lo = HALO - pad + tap
            acc = acc + u_scr[c, lo:lo + tm, :] * cw_half[tap:tap + 1, cs]
        parts.append(acc)
    qk = _silu_of_half(jnp.concatenate(parts, axis=1))
    q_ref[0] = qk[:, :A_WIDTH].astype(BF16)
    k = qk[:, A_WIDTH:] * (A_HEAD_DIM ** -0.5)
    kt_ref[0] = k.T.astype(BF16)

    va_ref[0] = _dot(hn, wv_ref[0]).astype(BF16)
    og_ref[0] = _sigmoid(_dot(hn, wo_ref[0])).astype(BF16)
    zg_ref[0] = _silu_of_half(0.5 * _dot(hn, wz_ref[0])).astype(BF16)

    bq = _dot(hn, wb_ref[0, :, :B_WIDTH])
    qb_ref[0] = (bq * (B_HEAD_DIM ** -0.5 * LOG2E)).astype(BF16)
    kv = _dot(hn, wb_ref[0, :, B_WIDTH:B_WIDTH + 2 * LANES])
    half = lax.broadcasted_iota(jnp.int32, (tm, LANES), 1) < B_HEAD_DIM
    for src, dst in ((kv[:, :LANES], kb2_ref), (kv[:, LANES:], vb2_ref)):
        sw = pltpu.roll(src, B_HEAD_DIM, 1)
        dst[0, :, :LANES] = jnp.where(half, src, sw).astype(BF16)
        dst[0, :, LANES:] = jnp.where(half, sw, src).astype(BF16)
    zb_ref[0] = _silu_of_half(0.5 * _dot(hn, wb_ref[0, :, B_WIDTH + 2 * LANES:])).astype(BF16)

    g = _dot_nt(wgt_ref[0], hn) + gb_ref[0]
    row = lax.broadcasted_iota(jnp.int32, g.shape, 0)
    gr_ref[0] = jnp.where(row < N_CHAN, g, _log_sigmoid(g))


def _in_proj(x, nw, w_all, wgt, wb, cw, cb, gb, *, tm, layer):
    bsz, seqlen, _ = x.shape
    nt = seqlen // tm
    hb = tm // HALO
    nhb = seqlen // HALO

    def full(arr):
        return _layer_spec(arr, layer)

    def w_cols(start, width):
        assert start % width == 0 and width % LANES == 0
        return pl.BlockSpec((1, D_MODEL, width), lambda b, i: (layer, 0, start // width))

    def rows(width):
        return pl.BlockSpec((1, tm, width), lambda b, i: (b, i, 0))

    out_shape = (
        jax.ShapeDtypeStruct((bsz, seqlen, A_WIDTH), BF16),
        jax.ShapeDtypeStruct((bsz, A_WIDTH, seqlen), BF16),
        jax.ShapeDtypeStruct((bsz, seqlen, A_WIDTH), BF16),
        jax.ShapeDtypeStruct((bsz, seqlen, A_WIDTH), BF16),
        jax.ShapeDtypeStruct((bsz, seqlen, A_WIDTH), BF16),
        jax.ShapeDtypeStruct((bsz, seqlen, B_WIDTH), BF16),
        jax.ShapeDtypeStruct((bsz, seqlen, 2 * LANES), BF16),
        jax.ShapeDtypeStruct((bsz, seqlen, 2 * LANES), BF16),
        jax.ShapeDtypeStruct((bsz, seqlen, B_WIDTH), BF16),
        jax.ShapeDtypeStruct((bsz, N_GATES, seqlen), F32),
    )
    out_specs = (
        rows(A_WIDTH),
        pl.BlockSpec((1, A_WIDTH, tm), lambda b, i: (b, 0, i)),
        rows(A_WIDTH), rows(A_WIDTH), rows(A_WIDTH), rows(B_WIDTH),
        rows(2 * LANES), rows(2 * LANES), rows(B_WIDTH),
        pl.BlockSpec((1, N_GATES, tm), lambda b, i: (b, 0, i)),
    )
    in_specs = [
        rows(D_MODEL),
        pl.BlockSpec((1, HALO, D_MODEL), lambda b, i: (b, jnp.maximum(i * hb - 1, 0), 0)),
        pl.BlockSpec((1, HALO, D_MODEL), lambda b, i: (b, jnp.minimum((i + 1) * hb, nhb - 1), 0)),
        full(nw),
        w_cols(0, 2 * A_WIDTH),
        w_cols(2 * A_WIDTH, A_WIDTH),
        w_cols(3 * A_WIDTH, A_WIDTH),
        w_cols(4 * A_WIDTH, A_WIDTH),
        full(wgt), full(wb), full(cw), full(cb), full(gb),
    ]
    return pl.pallas_call(
        functools.partial(_in_proj_kernel, tm=tm),
        grid=(bsz, nt),
        in_specs=in_specs,
        out_specs=out_specs,
        out_shape=out_shape,
        scratch_shapes=[pltpu.VMEM((2 * A_WIDTH // LANES, tm + 2 * HALO, LANES), F32)],
        compiler_params=_params("parallel", "arbitrary"),
        name="in_proj",
    )(x, x, x, nw, w_all, w_all, w_all, w_all, wgt, wb, cw, cb, gb)


GROUP_CHUNKS = LANES // N_CHAN


def _gate_prep_kernel(gr_ref, ar_ref, br_ref, cpk_ref, bpk_ref):
    ti = lax.broadcasted_iota(jnp.int32, (CHUNK, CHUNK), 0)
    si = lax.broadcasted_iota(jnp.int32, (CHUNK, CHUNK), 1)
    upper_f = (ti <= si).astype(F32)
    lower_f = (ti >= si).astype(F32)
    fwd_row = lax.rem(ti, N_CHAN) < A_HEADS
    li = jnp.concatenate([gr_ref[0, :N_CHAN, c * CHUNK:(c + 1) * CHUNK]
                          for c in range(GROUP_CHUNKS)], axis=0) * LOG2E
    lf = jnp.concatenate([gr_ref[0, N_CHAN:, c * CHUNK:(c + 1) * CHUNK]
                          for c in range(GROUP_CHUNKS)], axis=0) * LOG2E
    b_pre = jnp.dot(lf, upper_f, preferred_element_type=F32, precision=lax.Precision.HIGHEST)
    b_suf = jnp.dot(lf, lower_f, preferred_element_type=F32, precision=lax.Precision.HIGHEST)
    b = jnp.where(fwd_row, b_pre, b_suf)
    a = li - b
    pre, suf = a, a
    sh = 1
    while sh < CHUNK:
        pre = jnp.where(si >= sh, jnp.maximum(pre, pltpu.roll(pre, sh, 1)), pre)
        suf = jnp.where(si < CHUNK - sh, jnp.maximum(suf, pltpu.roll(suf, CHUNK - sh, 1)), suf)
        sh *= 2
    cmax = jnp.where(fwd_row, pre, suf)
    for c in range(GROUP_CHUNKS):
        rs = slice(c * N_CHAN, (c + 1) * N_CHAN)
        ar_ref[0, :, c * CHUNK:(c + 1) * CHUNK] = a[rs, :]
        br_ref[0, :, c * CHUNK:(c + 1) * CHUNK] = b[rs, :]
    cpk_ref[0, 0] = cmax.T
    bpk_ref[0, 0] = b.T


def _gate_prep(gr):
    bsz, _, seqlen = gr.shape
    tg = GROUP_CHUNKS * CHUNK
    assert seqlen % tg == 0
    row_spec = pl.BlockSpec((1, N_CHAN, tg), lambda b, i: (b, 0, i))
    pk_spec = pl.BlockSpec((1, 1, CHUNK, LANES), lambda b, i: (b, i, 0, 0))
    pk_shape = jax.ShapeDtypeStruct((bsz, seqlen // tg, CHUNK, LANES), F32)
    return pl.pallas_call(
        _gate_prep_kernel,
        grid=(bsz, seqlen // tg),
        in_specs=[pl.BlockSpec((1, N_GATES, tg), lambda b, i: (b, 0, i))],
        out_specs=(row_spec, row_spec, pk_spec, pk_spec),
        out_shape=(jax.ShapeDtypeStruct((bsz, N_CHAN, seqlen), F32),
                   jax.ShapeDtypeStruct((bsz, N_CHAN, seqlen), F32),
                   pk_shape, pk_shape),
        compiler_params=_params("parallel", "parallel"),
        name="gate_prep",
    )(gr)


CHUNKS_PER_STEP = 4


def _mlstm_kernel(qf_ref, qb_ref, ktf_ref, ktb_ref, vf_ref, vb_ref,
                  arf_ref, arb_ref, brf_ref, brb_ref, ccf_ref, ccb_ref, bcf_ref, bcb_ref,
                  hf_ref, hb_ref, c_scr, m_scr, *, bsz):
    j = pl.program_id(0)

    @pl.when(j == 0)
    def _():
        c_scr[...] = jnp.zeros_like(c_scr)
        m_scr[...] = jnp.zeros_like(m_scr)

    ti = lax.broadcasted_iota(jnp.int32, (CHUNK, CHUNK), 0)
    si = lax.broadcasted_iota(jnp.int32, (CHUNK, CHUNK), 1)
    ones_blk = jnp.ones((CHUNK, A_HEAD_DIM), BF16)
    sub8 = lax.broadcasted_iota(jnp.int32, (N_CHAN, LANES), 0)
    lane8 = lax.broadcasted_iota(jnp.int32, (N_CHAN, LANES), 1)

    n_steps = pl.num_programs(0)
    for k in range(CHUNKS_PER_STEP):
        _mlstm_chunk(j, k, n_steps, qf_ref, qb_ref, ktf_ref, ktb_ref, vf_ref, vb_ref,
                     arf_ref, arb_ref, brf_ref, brb_ref, ccf_ref, ccb_ref, bcf_ref, bcb_ref,
                     hf_ref, hb_ref, c_scr, m_scr, bsz, ti, si, ones_blk, sub8, lane8)


def _mlstm_chunk(j, k, n_steps, qf_ref, qb_ref, ktf_ref, ktb_ref, vf_ref, vb_ref,
                 arf_ref, arb_ref, brf_ref, brb_ref, ccf_ref, ccb_ref, bcf_ref, bcb_ref,
                 hf_ref, hb_ref, c_scr, m_scr, bsz, ti, si, ones_blk, sub8, lane8):
    jf = j * CHUNKS_PER_STEP + k
    jb = n_steps * CHUNKS_PER_STEP - 1 - jf
    base_f = lax.rem(jf, GROUP_CHUNKS) * N_CHAN
    base_b = lax.rem(jb, GROUP_CHUNKS) * N_CHAN
    rows_f = slice(k * CHUNK, (k + 1) * CHUNK)
    rows_b = slice((CHUNKS_PER_STEP - 1 - k) * CHUNK, (CHUNKS_PER_STEP - k) * CHUNK)
    dirs = (
        (0, qf_ref, ktf_ref, vf_ref, arf_ref, brf_ref, ccf_ref, bcf_ref, hf_ref, ti >= si, CHUNK - 1,
         base_f, rows_f),
        (1, qb_ref, ktb_ref, vb_ref, arb_ref, brb_ref, ccb_ref, bcb_ref, hb_ref, ti <= si, 0,
         base_b, rows_b),
    )

    def body(b, carry):
        tiles = []
        for d, q_ref, kt_ref, v_ref, ar_ref, br_ref, cc_ref, bc_ref, h_ref, vis, last, base, rows in dirs:
            for h in range(A_HEADS):
                hs = slice(h * A_HEAD_DIM, (h + 1) * A_HEAD_DIM)
                idx = (b * 2 + d) * A_HEADS + h
                q = q_ref[b, rows, hs]
                kt = kt_ref[b, hs, rows]
                tiles.append((_dot(q, kt), q, kt, idx, hs))

        gates = []
        for d, q_ref, kt_ref, v_ref, ar_ref, br_ref, cc_ref, bc_ref, h_ref, vis, last, base, rows in dirs:
            a_row = ar_ref[b, :, rows]
            b_last = br_ref[b, :, rows][:, last:last + 1]
            m8 = m_scr[b * 2 + d]
            gl8 = jnp.maximum(m8, jnp.max(a_row, axis=1, keepdims=True))
            w_row = jnp.exp2(a_row - gl8)
            decay8 = jnp.exp2(m8 - gl8)
            m_scr[b * 2 + d] = b_last + gl8
            m_lane = jnp.sum(jnp.where(sub8 == lane8, m8, 0.0), axis=0, keepdims=True)
            unrot = lax.rem(LANES - base, LANES)
            g = jnp.maximum(m_lane, pltpu.roll(cc_ref[b, 0], unrot, 1))
            emt = jnp.exp2(-(pltpu.roll(bc_ref[b, 0], unrot, 1) + g))
            gates.append((a_row, w_row, decay8, m8, g, emt))

        updates = []
        for d, q_ref, kt_ref, v_ref, ar_ref, br_ref, cc_ref, bc_ref, h_ref, vis, last, base, rows in dirs:
            a_row, w_row, decay8, m8, g, emt = gates[d]
            for h in range(A_HEADS):
                ch = d * A_HEADS + h
                qk, q, kt, idx, hs = tiles[ch]
                c_old = c_scr[idx]
                v_aug = jnp.concatenate([v_ref[b, rows, hs], ones_blk], axis=1)
                g_b = jnp.broadcast_to(g[:, ch:ch + 1], (CHUNK, LANES))
                p = jnp.where(vis, jnp.exp2(a_row[ch:ch + 1, :] - g_b), 0.0)
                iw_b = jnp.exp2(m8[ch:ch + 1, :] - g_b)
                lhs = jnp.concatenate([(qk * p).astype(BF16),
                                       (q.astype(F32) * iw_b).astype(BF16)], axis=1)
                rhs = jnp.concatenate([v_aug, c_old.astype(BF16)], axis=0)
                r = _dot(lhs, rhs)
                num = r[:, :A_HEAD_DIM]
                den = r[:, A_HEAD_DIM:]
                h_ref[b, rows, hs] = (num / jnp.maximum(jnp.abs(den), emt[:, ch:ch + 1])).astype(BF16)
                ktw = (kt.astype(F32) * w_row[ch:ch + 1, :]).astype(BF16)
                updates.append((idx, decay8[ch:ch + 1, :1] * c_old, ktw, v_aug))

        for idx, c_dec, ktw, v_aug in updates:
            c_scr[idx] = c_dec + _dot(ktw, v_aug)
        return carry

    lax.fori_loop(0, bsz, body, 0, unroll=True)


def _mlstm(q, kt, va, ar, br, cc, bc):
    bsz, seqlen, _ = q.shape
    nc = seqlen // CHUNK
    cps = CHUNKS_PER_STEP
    assert nc % cps == 0 and GROUP_CHUNKS % cps == 0
    ns = nc // cps
    fwd3 = lambda j: (0, j, 0)
    bwd3 = lambda j: (0, ns - 1 - j, 0)
    fwd3t = lambda j: (0, 0, j)
    bwd3t = lambda j: (0, 0, ns - 1 - j)
    tok = (bsz, cps * CHUNK, A_WIDTH)
    tok_t = (bsz, A_WIDTH, cps * CHUNK)
    rowb = (bsz, N_CHAN, cps * CHUNK)
    colb = (bsz, 1, CHUNK, LANES)
    fwd4 = lambda j: (0, (j * cps) // GROUP_CHUNKS, 0, 0)
    bwd4 = lambda j: (0, (nc - 1 - j * cps) // GROUP_CHUNKS, 0, 0)
    in_specs = [
        pl.BlockSpec(tok, fwd3), pl.BlockSpec(tok, bwd3),
        pl.BlockSpec(tok_t, fwd3t), pl.BlockSpec(tok_t, bwd3t),
        pl.BlockSpec(tok, fwd3), pl.BlockSpec(tok, bwd3),
        pl.BlockSpec(rowb, fwd3t), pl.BlockSpec(rowb, bwd3t),
        pl.BlockSpec(rowb, fwd3t), pl.BlockSpec(rowb, bwd3t),
        pl.BlockSpec(colb, fwd4), pl.BlockSpec(colb, bwd4),
        pl.BlockSpec(colb, fwd4), pl.BlockSpec(colb, bwd4),
    ]
    return pl.pallas_call(
        functools.partial(_mlstm_kernel, bsz=bsz),
        grid=(ns,),
        in_specs=in_specs,
        out_specs=(pl.BlockSpec(tok, fwd3), pl.BlockSpec(tok, bwd3)),
        out_shape=(jax.ShapeDtypeStruct((bsz, seqlen, A_WIDTH), BF16),
                   jax.ShapeDtypeStruct((bsz, seqlen, A_WIDTH), BF16)),
        scratch_shapes=[pltpu.VMEM((bsz * 2 * A_HEADS, A_HEAD_DIM, 2 * A_HEAD_DIM), F32),
                        pltpu.VMEM((bsz * 2, N_CHAN, LANES), F32)],
        compiler_params=_params("arbitrary"),
        name="mlstm",
    )(q, q, kt, kt, va, va, ar, ar, br, br, cc, cc, bc, bc)


def _t5_bucket(rel):
    nb = N_BUCKETS // 2
    max_exact = nb // 2
    ret = jnp.where(rel > 0, nb, 0)
    n = jnp.abs(rel)
    nf = jnp.maximum(n, 1).astype(jnp.float32)
    large = max_exact + (jnp.log(nf / max_exact) / math.log(MAX_DISTANCE / max_exact)
                         * (nb - max_exact)).astype(jnp.int32)
    large = jnp.minimum(large, nb - 1)
    return ret + jnp.where(n < max_exact, n, large)


def _bias_kernel(rb_ref, bucket_ref, bias_ref):
    bucket = bucket_ref[...]
    qi = lax.broadcasted_iota(jnp.int32, bucket.shape, 0)
    kj = lax.broadcasted_iota(jnp.int32, bucket.shape, 1)
    band = jnp.abs(kj - BLOCK - qi) <= WINDOW
    masks = (band & (kj >= BLOCK), band, band & (kj < 2 * BLOCK))
    for hq in range(B_Q_HEADS):
        acc = jnp.zeros(bucket.shape, F32)
        for nb in range(N_BUCKETS):
            acc = jnp.where(bucket == nb, rb_ref[nb, hq], acc)
        acc = acc * LOG2E
        for v, mask in enumerate(masks):
            bias_ref[v, hq] = jnp.where(mask, acc, NEG_INF)


def _bias_table(rel_bias):
    q_off = jnp.arange(BLOCK)
    k_off = jnp.arange(3 * BLOCK) - BLOCK
    bucket = _t5_bucket(k_off[None, :] - q_off[:, None]).astype(jnp.int32)
    shape = (3, B_Q_HEADS, BLOCK, 3 * BLOCK)
    return pl.pallas_call(
        _bias_kernel,
        in_specs=[pl.BlockSpec(memory_space=pltpu.SMEM),
                  pl.BlockSpec(bucket.shape, lambda: (0, 0))],
        out_specs=pl.BlockSpec(shape, lambda: (0, 0, 0, 0)),
        out_shape=jax.ShapeDtypeStruct(shape, F32),
        name="bias_table",
    )(rel_bias.astype(F32), bucket)


def _attn_out_kernel(sink_ref, q_ref, kp_ref, kc_ref, kn_ref, vp_ref, vc_ref, vn_ref, zb_ref,
                     bias_ref, hf_ref, hb_ref, og_ref, zg_ref, x_ref, mw_ref, wo_ref, fw_ref,
                     o_ref, s_scr, yb_scr, *, n_qb, layer, final):
    j = pl.program_id(1)
    last = pl.num_programs(1) - 1
    n_keys = (n_qb + 2) * BLOCK
    lo_k = lax.broadcasted_iota(jnp.int32, (n_keys, LANES), 1) < B_HEAD_DIM
    lo_q = lax.broadcasted_iota(jnp.int32, (BLOCK, LANES), 1) < B_HEAD_DIM
    zero = jnp.zeros((n_keys, LANES), BF16)
    k_sel, v_sel = [], []
    for h in range(B_KV_HEADS):
        hs = slice(h * LANES, (h + 1) * LANES)
        k2 = jnp.concatenate([kp_ref[0, :, hs], kc_ref[0, :, hs], kn_ref[0, :, hs]], axis=0)
        v2 = jnp.concatenate([vp_ref[0, :, hs], vc_ref[0, :, hs], vn_ref[0, :, hs]], axis=0)
        k_sel.append((jnp.where(lo_k, k2, zero), jnp.where(lo_k, zero, k2)))
        v_sel.append((jnp.where(lo_k, v2, zero), jnp.where(lo_k, zero, v2)))

    for i in range(n_qb):
        variant = jnp.int32(1)
        if i == 0:
            variant = jnp.where(j == 0, 0, variant)
        if i == n_qb - 1:
            variant = jnp.where(j == last, 2, variant)
        rows = slice(i * BLOCK, (i + 1) * BLOCK)
        win = slice(i * BLOCK, (i + 3) * BLOCK)
        for p_idx in range(B_Q_HEADS // 2):
            qp = q_ref[0, rows, p_idx * LANES:(p_idx + 1) * LANES]
            for par in range(2):
                hq = p_idx * 2 + par
                k_win = k_sel[p_idx // 2][par][win]
                s_scr[i * B_Q_HEADS + hq] = _dot_nt(qp, k_win) + bias_ref[variant, hq]

    for i in range(n_qb):
        rows = slice(i * BLOCK, (i + 1) * BLOCK)
        win = slice(i * BLOCK, (i + 3) * BLOCK)
        for p_idx in range(B_Q_HEADS // 2):
            ps = slice(p_idx * LANES, (p_idx + 1) * LANES)
            probs, dens = [], []
            for par in range(2):
                hq = p_idx * 2 + par
                sink = sink_ref[layer, hq] * LOG2E
                sc = s_scr[i * B_Q_HEADS + hq]
                m = jnp.maximum(jnp.max(sc, axis=-1, keepdims=True), sink)
                p = jnp.exp2(sc - m)
                dens.append(jnp.sum(p, axis=-1, keepdims=True) + jnp.exp2(sink - m))
                probs.append(p.astype(BF16))
            v_even, v_odd = v_sel[p_idx // 2]
            v_bd = jnp.concatenate([v_even[win], v_odd[win]], axis=0)
            out = _dot(jnp.concatenate(probs, axis=1), v_bd)
            y = out / jnp.where(lo_q, dens[0], dens[1])
            yb_scr[rows, ps] = (y * zb_ref[0, rows, ps].astype(F32)).astype(BF16)

    h = og_ref[0].astype(F32) * (hf_ref[0].astype(F32) + hb_ref[0].astype(F32))
    parts = []
    for k in range(A_HEADS):
        hs = slice(k * A_HEAD_DIM, (k + 1) * A_HEAD_DIM)
        hh = h[:, hs]
        hh = hh * lax.rsqrt(jnp.mean(hh * hh, axis=-1, keepdims=True) + EPS)
        parts.append(hh * mw_ref[0, :, hs])
    ya = (jnp.concatenate(parts, axis=1) * zg_ref[0].astype(F32)).astype(BF16)
    out = x_ref[0] + _dot(ya, wo_ref[0, :A_WIDTH, :]) + _dot(yb_scr[...], wo_ref[0, A_WIDTH:, :])
    if final:
        out = out * lax.rsqrt(jnp.mean(out * out, axis=-1, keepdims=True) + EPS) * fw_ref[...]
    o_ref[0] = out


def _attn_out(sink, qb, kb2, vb2, zb, bias, hf, hb, og, zg, x, mw, wo, fw, *, n_qb, layer, final):
    bsz, seqlen, _ = qb.shape
    nb = seqlen // BLOCK
    assert nb >= 2 and nb % n_qb == 0
    tq = n_qb * BLOCK
    cur = lambda b, j: (b, j, 0)
    prev = lambda b, j: (b, jnp.maximum(j * n_qb - 1, 0), 0)
    nxt = lambda b, j: (b, jnp.minimum((j + 1) * n_qb, nb - 1), 0)
    halo = pl.BlockSpec((1, BLOCK, 2 * LANES), prev), pl.BlockSpec((1, BLOCK, 2 * LANES), nxt)
    kv_cur = pl.BlockSpec((1, tq, 2 * LANES), cur)

    def tok(width):
        return pl.BlockSpec((1, tq, width), cur)

    def full(arr):
        return pl.BlockSpec(arr.shape, lambda b, j: (0,) * arr.ndim)

    return pl.pallas_call(
        functools.partial(_attn_out_kernel, n_qb=n_qb, layer=layer, final=final),
        grid=(bsz, nb // n_qb),
        in_specs=[pl.BlockSpec(memory_space=pltpu.SMEM),
                  tok(B_WIDTH),
                  halo[0], kv_cur, halo[1],
                  halo[0], kv_cur, halo[1],
                  tok(B_WIDTH), full(bias),
                  tok(A_WIDTH), tok(A_WIDTH), tok(A_WIDTH), tok(A_WIDTH), tok(D_MODEL),
                  _layer_spec(mw, layer), _layer_spec(wo, layer), full(fw)],
        out_specs=tok(D_MODEL),
        out_shape=jax.ShapeDtypeStruct(x.shape, F32),
        scratch_shapes=[pltpu.VMEM((n_qb * B_Q_HEADS, BLOCK, 3 * BLOCK), F32),
                        pltpu.VMEM((tq, B_WIDTH), BF16)],
        compiler_params=_params("parallel", "parallel"),
        name="attn_out",
    )(sink, qb, kb2, kb2, kb2, vb2, vb2, vb2, zb, bias, hf, hb, og, zg, x, mw, wo, fw)


def _tile(n, target):
    t = min(n, target)
    assert n % t == 0, (n, t)
    return t


def kernel(x, norm_w, w_in, conv_w, conv_b, gate_b, mhn_w, sink, rel_bias, w_out, final_norm_w):
    bsz, seqlen, d_model = x.shape
    depth = norm_w.shape[0]
    assert d_model == D_MODEL and seqlen % CHUNK == 0
    tm = _tile(seqlen, 1024)
    n_qb = _tile(seqlen // BLOCK, 4)

    c_qk = 2 * A_WIDTH
    c_a = c_qk + 3 * A_WIDTH
    c_g = c_a + N_GATES
    assert c_a == 5 * A_WIDTH
    w_bf = w_in.astype(BF16)
    wgt = jnp.swapaxes(w_bf[:, :, c_a:c_g], 1, 2)
    wb = w_bf[:, :, c_g:]
    wo = w_out.astype(BF16)
    cw = jnp.pad(conv_w, ((0, 0), (0, SUBLANES - CONV_K), (0, 0)))
    nw = norm_w.reshape(depth, 1, D_MODEL)
    cb = conv_b.reshape(depth, 1, 2 * A_WIDTH)
    gb = gate_b.reshape(depth, N_GATES, 1)
    mw = mhn_w.reshape(depth, 1, A_WIDTH)
    bias = _bias_table(rel_bias)
    fw = final_norm_w.reshape(1, D_MODEL)

    xf = x
    for l in range(depth):
        q, kt, va, og, zg, qb, kb2, vb2, zb, gr = _in_proj(
            xf, nw, w_bf, wgt, wb, cw, cb, gb, tm=tm, layer=l)
        ar, br, cc, bc = _gate_prep(gr)
        hf, hb = _mlstm(q, kt, va, ar, br, cc, bc)
        xf = _attn_out(sink, qb, kb2, vb2, zb, bias, hf, hb, og, zg, xf, mw, wo, fw,
                       n_qb=n_qb, layer=l, final=(l == depth - 1))
    return xf
```

```python
import functools
import math

import jax
import jax.numpy as jnp
import numpy as np
from jax import lax
from jax.experimental import pallas as pl
from jax.experimental.pallas import tpu as pltpu

D_MODEL = 1024
A_WIDTH = 512
A_HEADS = 4
A_HEAD_DIM = 128
CHUNK = 128
CONV_K = 5
B_WIDTH = 512
B_HEAD_DIM = 64
B_Q_HEADS = 8
B_KV_HEADS = 2
WINDOW = 128
BLOCK = 128
N_BUCKETS = 32
MAX_DISTANCE = 128
EPS = 1e-6
NEG_INF = -1e30
LOG2E = math.log2(math.e)
N_GATES = 4 * A_HEADS
N_CHAN = 2 * A_HEADS

LANES = 128
SUBLANES = 8
VMEM_LIMIT_BYTES = 56 * 1024 * 1024

HALO = 2 * SUBLANES
BF16 = jnp.bfloat16
F32 = jnp.float32


def _params(*sem):
    return pltpu.CompilerParams(dimension_semantics=sem, vmem_limit_bytes=VMEM_LIMIT_BYTES)


def _dot(a, b):
    return jnp.dot(a, b, preferred_element_type=F32)


def _dot_nt(a, b):
    return lax.dot_general(a, b, (((1,), (1,)), ((), ())), preferred_element_type=F32)


def _log_sigmoid(x):
    return -(jnp.maximum(-x, 0.0) + jnp.log1p(jnp.exp(-jnp.abs(x))))


def _sigmoid(x):
    return 0.5 * jnp.tanh(0.5 * x) + 0.5


def _silu_of_half(h):
    return h + h * jnp.tanh(h)


def _layer_spec(arr, layer):
    return pl.BlockSpec((1,) + arr.shape[1:], lambda *_: (layer,) + (0,) * (arr.ndim - 1))


def _in_proj_kernel(x_ref, xp_ref, xn_ref, nw_ref, wqk_ref, wv_ref, wo_ref, wz_ref, wgt_ref, wb_ref,
                    cw_ref, cb_ref, gb_ref,
                    q_ref, kt_ref, va_ref, og_ref, zg_ref, qb_ref, kb2_ref, vb2_ref, zb_ref, gr_ref,
                    u_scr, *, tm):
    i = pl.program_id(1)
    last = pl.num_programs(1) - 1
    nw = nw_ref[0]

    def norm(xv):
        y = xv * lax.rsqrt(jnp.mean(xv * xv, axis=-1, keepdims=True) + EPS)
        return (y * nw).astype(BF16)

    hn = norm(x_ref[0])
    g = _dot_nt(wgt_ref[0], hn) + gb_ref[0]
    row = lax.broadcasted_iota(jnp.int32, g.shape, 0)
    gr_ref[0] = jnp.where(row < N_CHAN, g, _log_sigmoid(g))
    hp = norm(xp_ref[0])
    hx = norm(xn_ref[0])
    u_all = _dot(jnp.concatenate([hp, hn, hx], axis=0), wqk_ref[0])
    u_top = jnp.where(i == 0, 0.0, u_all[:HALO])
    u_bot = jnp.where(i == last, 0.0, u_all[HALO + tm:])
    cw_half = 0.5 * cw_ref[0]
    cb_half = 0.5 * cb_ref[0]
    parts = []
    pad = CONV_K // 2
    for c in range(2 * A_WIDTH // LANES):
        cs = slice(c * LANES, (c + 1) * LANES)
        u_scr[c, :HALO, :] = u_top[:, cs]
        u_scr[c, HALO:HALO + tm, :] = u_all[HALO:HALO + tm, cs]
        u_scr[c, HALO + tm:, :] = u_bot[:, cs]
        acc = cb_half[:, cs]
        for tap in range(CONV_K):
            lo = HALO - pad + tap
            acc = acc + u_scr[c, lo:lo + tm, :] * cw_half[tap:tap + 1, cs]
        parts.append(acc)
    qk = _silu_of_half(jnp.concatenate(parts, axis=1))
    q_ref[0] = qk[:, :A_WIDTH].astype(BF16)
    k = qk[:, A_WIDTH:] * (A_HEAD_DIM ** -0.5)
    kt_ref[0] = k.T.astype(BF16)

    va_ref[0] = _dot(hn, wv_ref[0]).astype(BF16)
    og_ref[0] = _sigmoid(_dot(hn, wo_ref[0])).astype(BF16)
    zg_ref[0] = _silu_of_half(0.5 * _dot(hn, wz_ref[0])).astype(BF16)

    bq = _dot(hn, wb_ref[0, :, :B_WIDTH])
    qb_ref[0] = (bq * (B_HEAD_DIM ** -0.5 * LOG2E)).astype(BF16)
    kv = _dot(hn, wb_ref[0, :, B_WIDTH:B_WIDTH + 2 * LANES])
    half = lax.broadcasted_iota(jnp.int32, (tm, LANES), 1) < B_HEAD_DIM
    for src, dst in ((kv[:, :LANES], kb2_ref), (kv[:, LANES:], vb2_ref)):
        sw = pltpu.roll(src, B_HEAD_DIM, 1)
        dst[0, :, :LANES] = jnp.where(half, src, sw).astype(BF16)
        dst[0, :, LANES:] = jnp.where(half, sw, src).astype(BF16)
    zb_ref[0] = _silu_of_half(0.5 * _dot(hn, wb_ref[0, :, B_WIDTH + 2 * LANES:])).astype(BF16)


def _in_proj(x, nw, w_all, wgt, wb, cw, cb, gb, *, tm, layer):
    bsz, seqlen, _ = x.shape
    nt = seqlen // tm
    hb = tm // HALO
    nhb = seqlen // HALO

    def full(arr):
        return _layer_spec(arr, layer)

    def w_cols(start, width):
        assert start % width == 0 and width % LANES == 0
        return pl.BlockSpec((1, D_MODEL, width), lambda b, i: (layer, 0, start // width))

    def rows(width):
        return pl.BlockSpec((1, tm, width), lambda b, i: (b, i, 0))

    out_shape = (
        jax.ShapeDtypeStruct((bsz, seqlen, A_WIDTH), BF16),
        jax.ShapeDtypeStruct((bsz, A_WIDTH, seqlen), BF16),
        jax.ShapeDtypeStruct((bsz, seqlen, A_WIDTH), BF16),
        jax.ShapeDtypeStruct((bsz, seqlen, A_WIDTH), BF16),
        jax.ShapeDtypeStruct((bsz, seqlen, A_WIDTH), BF16),
        jax.ShapeDtypeStruct((bsz, seqlen, B_WIDTH), BF16),
        jax.ShapeDtypeStruct((bsz, seqlen, 2 * LANES), BF16),
        jax.ShapeDtypeStruct((bsz, seqlen, 2 * LANES), BF16),
        jax.ShapeDtypeStruct((bsz, seqlen, B_WIDTH), BF16),
        jax.ShapeDtypeStruct((bsz, N_GATES, seqlen), F32),
    )
    out_specs = (
        rows(A_WIDTH),
        pl.BlockSpec((1, A_WIDTH, tm), lambda b, i: (b, 0, i)),
        rows(A_WIDTH), rows(A_WIDTH), rows(A_WIDTH), rows(B_WIDTH),
        rows(2 * LANES), rows(2 * LANES), rows(B_WIDTH),
        pl.BlockSpec((1, N_GATES, tm), lambda b, i: (b, 0, i)),
    )
    in_specs = [
        rows(D_MODEL),
        pl.BlockSpec((1, HALO, D_MODEL), lambda b, i: (b, jnp.maximum(i * hb - 1, 0), 0)),
        pl.BlockSpec((1, HALO, D_MODEL), lambda b, i: (b, jnp.minimum((i + 1) * hb, nhb - 1), 0)),
        full(nw),
        w_cols(0, 2 * A_WIDTH),
        w_cols(2 * A_WIDTH, A_WIDTH),
        w_cols(3 * A_WIDTH, A_WIDTH),
        w_cols(4 * A_WIDTH, A_WIDTH),
        full(wgt), full(wb), full(cw), full(cb), full(gb),
    ]
    return pl.pallas_call(
        functools.partial(_in_proj_kernel, tm=tm),
        grid=(bsz, nt),
        in_specs=in_specs,
        out_specs=out_specs,
        out_shape=out_shape,
        scratch_shapes=[pltpu.VMEM((2 * A_WIDTH // LANES, tm + 2 * HALO, LANES), F32)],
        compiler_params=_params("parallel", "arbitrary"),
        name="in_proj",
    )(x, x, x, nw, w_all, w_all, w_all, w_all, wgt, wb, cw, cb, gb)


GROUP_CHUNKS = LANES // N_CHAN


def _gate_prep_kernel(gr_ref, ar_ref, br_ref, cpk_ref, bpk_ref):
    ti = lax.broadcasted_iota(jnp.int32, (CHUNK, CHUNK), 0)
    si = lax.broadcasted_iota(jnp.int32, (CHUNK, CHUNK), 1)
    upper_f = (ti <= si).astype(F32)
    lower_f = (ti >= si).astype(F32)
    fwd_row = lax.rem(ti, N_CHAN) < A_HEADS
    li = jnp.concatenate([gr_ref[0, :N_CHAN, c * CHUNK:(c + 1) * CHUNK]
                          for c in range(GROUP_CHUNKS)], axis=0) * LOG2E
    lf = jnp.concatenate([gr_ref[0, N_CHAN:, c * CHUNK:(c + 1) * CHUNK]
                          for c in range(GROUP_CHUNKS)], axis=0) * LOG2E
    b_pre = jnp.dot(lf, upper_f, preferred_element_type=F32, precision=lax.Precision.HIGHEST)
    b_suf = jnp.dot(lf, lower_f, preferred_element_type=F32, precision=lax.Precision.HIGHEST)
    b = jnp.where(fwd_row, b_pre, b_suf)
    a = li - b
    pre, suf = a, a
    sh = 1
    while sh < CHUNK:
        pre = jnp.where(si >= sh, jnp.maximum(pre, pltpu.roll(pre, sh, 1)), pre)
        suf = jnp.where(si < CHUNK - sh, jnp.maximum(suf, pltpu.roll(suf, CHUNK - sh, 1)), suf)
        sh *= 2
    cmax = jnp.where(fwd_row, pre, suf)
    for c in range(GROUP_CHUNKS):
        rs = slice(c * N_CHAN, (c + 1) * N_CHAN)
        ar_ref[0, :, c * CHUNK:(c + 1) * CHUNK] = a[rs, :]
        br_ref[0, :, c * CHUNK:(c + 1) * CHUNK] = b[rs, :]
    cpk_ref[0, 0] = cmax.T
    bpk_ref[0, 0] = b.T


def _gate_prep(gr):
    bsz, _, seqlen = gr.shape
    tg = GROUP_CHUNKS * CHUNK
    assert seqlen % tg == 0
    row_spec = pl.BlockSpec((1, N_CHAN, tg), lambda b, i: (b, 0, i))
    pk_spec = pl.BlockSpec((1, 1, CHUNK, LANES), lambda b, i: (b, i, 0, 0))
    pk_shape = jax.ShapeDtypeStruct((bsz, seqlen // tg, CHUNK, LANES), F32)
    return pl.pallas_call(
        _gate_prep_kernel,
        grid=(bsz, seqlen // tg),
        in_specs=[pl.BlockSpec((1, N_GATES, tg), lambda b, i: (b, 0, i))],
        out_specs=(row_spec, row_spec, pk_spec, pk_spec),
        out_shape=(jax.ShapeDtypeStruct((bsz, N_CHAN, seqlen), F32),
                   jax.ShapeDtypeStruct((bsz, N_CHAN, seqlen), F32),
                   pk_shape, pk_shape),
        compiler_params=_params("parallel", "parallel"),
        name="gate_prep",
    )(gr)


CHUNKS_PER_STEP = 4


def _mlstm_kernel(qf_ref, qb_ref, ktf_ref, ktb_ref, vf_ref, vb_ref,
                  arf_ref, arb_ref, brf_ref, brb_ref, ccf_ref, ccb_ref, bcf_ref, bcb_ref,
                  hf_ref, hb_ref, c_scr, m_scr, *, bsz):
    j = pl.program_id(0)

    @pl.when(j == 0)
    def _():
        c_scr[...] = jnp.zeros_like(c_scr)
        m_scr[...] = jnp.zeros_like(m_scr)

    ti = lax.broadcasted_iota(jnp.int32, (CHUNK, CHUNK), 0)
    si = lax.broadcasted_iota(jnp.int32, (CHUNK, CHUNK), 1)
    ones_blk = jnp.ones((CHUNK, A_HEAD_DIM), BF16)
    sub8 = lax.broadcasted_iota(jnp.int32, (N_CHAN, LANES), 0)
    lane8 = lax.broadcasted_iota(jnp.int32, (N_CHAN, LANES), 1)

    n_steps = pl.num_programs(0)
    for k in range(CHUNKS_PER_STEP):
        _mlstm_chunk(j, k, n_steps, qf_ref, qb_ref, ktf_ref, ktb_ref, vf_ref, vb_ref,
                     arf_ref, arb_ref, brf_ref, brb_ref, ccf_ref, ccb_ref, bcf_ref, bcb_ref,
                     hf_ref, hb_ref, c_scr, m_scr, bsz, ti, si, ones_blk, sub8, lane8)


def _mlstm_chunk(j, k, n_steps, qf_ref, qb_ref, ktf_ref, ktb_ref, vf_ref, vb_ref,
                 arf_ref, arb_ref, brf_ref, brb_ref, ccf_ref, ccb_ref, bcf_ref, bcb_ref,
                 hf_ref, hb_ref, c_scr, m_scr, bsz, ti, si, ones_blk, sub8, lane8):
    jf = j * CHUNKS_PER_STEP + k
    jb = n_steps * CHUNKS_PER_STEP - 1 - jf
    base_f = lax.rem(jf, GROUP_CHUNKS) * N_CHAN
    base_b = lax.rem(jb, GROUP_CHUNKS) * N_CHAN
    rows_f = slice(k * CHUNK, (k + 1) * CHUNK)
    rows_b = slice((CHUNKS_PER_STEP - 1 - k) * CHUNK, (CHUNKS_PER_STEP - k) * CHUNK)
    dirs = (
        (0, qf_ref, ktf_ref, vf_ref, arf_ref, brf_ref, ccf_ref, bcf_ref, hf_ref, ti >= si, CHUNK - 1,
         base_f, rows_f),
        (1, qb_ref, ktb_ref, vb_ref, arb_ref, brb_ref, ccb_ref, bcb_ref, hb_ref, ti <= si, 0,
         base_b, rows_b),
    )

    def body(b, carry):
        tiles = []
        for d, q_ref, kt_ref, v_ref, ar_ref, br_ref, cc_ref, bc_ref, h_ref, vis, last, base, rows in dirs:
            for h in range(A_HEADS):
                hs = slice(h * A_HEAD_DIM, (h + 1) * A_HEAD_DIM)
                idx = (b * 2 + d) * A_HEADS + h
                q = q_ref[b, rows, hs]
                kt = kt_ref[b, hs, rows]
                tiles.append((_dot(q, kt), q, kt, idx, hs))

        gates = []
        for d, q_ref, kt_ref, v_ref, ar_ref, br_ref, cc_ref, bc_ref, h_ref, vis, last, base, rows in dirs:
            a_row = ar_ref[b, :, rows]
            b_last = br_ref[b, :, rows][:, last:last + 1]
            m8 = m_scr[b * 2 + d]
            gl8 = jnp.maximum(m8, jnp.max(a_row, axis=1, keepdims=True))
            w_row = jnp.exp2(a_row - gl8)
            decay8 = jnp.exp2(m8 - gl8)
            m_scr[b * 2 + d] = b_last + gl8
            m_lane = jnp.sum(jnp.where(sub8 == lane8, m8, 0.0), axis=0, keepdims=True)
            unrot = lax.rem(LANES - base, LANES)
            g = jnp.maximum(m_lane, pltpu.roll(cc_ref[b, 0], unrot, 1))
            emt = jnp.exp2(-(pltpu.roll(bc_ref[b, 0], unrot, 1) + g))
            gates.append((a_row, w_row, decay8, m8, g, emt))

        for d, q_ref, kt_ref, v_ref, ar_ref, br_ref, cc_ref, bc_ref, h_ref, vis, last, base, rows in dirs:
            a_row, w_row, decay8, m8, g, emt = gates[d]
            for h in range(A_HEADS):
                ch = d * A_HEADS + h
                qk, q, kt, idx, hs = tiles[ch]
                c_old = c_scr[idx]
                v_aug = jnp.concatenate([v_ref[b, rows, hs], ones_blk], axis=1)
                g_b = jnp.broadcast_to(g[:, ch:ch + 1], (CHUNK, LANES))
                p = jnp.where(vis, jnp.exp2(a_row[ch:ch + 1, :] - g_b), 0.0)
                iw_b = jnp.exp2(m8[ch:ch + 1, :] - g_b)
                lhs = jnp.concatenate([(qk * p).astype(BF16),
                                       (q.astype(F32) * iw_b).astype(BF16)], axis=1)
                rhs = jnp.concatenate([v_aug, c_old.astype(BF16)], axis=0)
                r = _dot(lhs, rhs)
                num = r[:, :A_HEAD_DIM]
                den = r[:, A_HEAD_DIM:]
                h_ref[b, rows, hs] = (num / jnp.maximum(jnp.abs(den), emt[:, ch:ch + 1])).astype(BF16)
                ktw = (kt.astype(F32) * w_row[ch:ch + 1, :]).astype(BF16)
                c_scr[idx] = decay8[ch:ch + 1, :1] * c_old + _dot(ktw, v_aug)
        return carry

    lax.fori_loop(0, bsz, body, 0, unroll=True)


def _mlstm(q, kt, va, ar, br, cc, bc):
    bsz, seqlen, _ = q.shape
    nc = seqlen // CHUNK
    cps = CHUNKS_PER_STEP
    assert nc % cps == 0 and GROUP_CHUNKS % cps == 0
    ns = nc // cps
    fwd3 = lambda j: (0, j, 0)
    bwd3 = lambda j: (0, ns - 1 - j, 0)
    fwd3t = lambda j: (0, 0, j)
    bwd3t = lambda j: (0, 0, ns - 1 - j)
    tok = (bsz, cps * CHUNK, A_WIDTH)
    tok_t = (bsz, A_WIDTH, cps * CHUNK)
    rowb = (bsz, N_CHAN, cps * CHUNK)
    colb = (bsz, 1, CHUNK, LANES)
    fwd4 = lambda j: (0, (j * cps) // GROUP_CHUNKS, 0, 0)
    bwd4 = lambda j: (0, (nc - 1 - j * cps) // GROUP_CHUNKS, 0, 0)
    in_specs = [
        pl.BlockSpec(tok, fwd3), pl.BlockSpec(tok, bwd3),
        pl.BlockSpec(tok_t, fwd3t), pl.BlockSpec(tok_t, bwd3t),
        pl.BlockSpec(tok, fwd3), pl.BlockSpec(tok, bwd3),
        pl.BlockSpec(rowb, fwd3t), pl.BlockSpec(rowb, bwd3t),
        pl.BlockSpec(rowb, fwd3t), pl.BlockSpec(rowb, bwd3t),
        pl.BlockSpec(colb, fwd4), pl.BlockSpec(colb, bwd4),
        pl.BlockSpec(colb, fwd4), pl.BlockSpec(colb, bwd4),
    ]
    return pl.pallas_call(
        functools.partial(_mlstm_kernel, bsz=bsz),
        grid=(ns,),
        in_specs=in_specs,
        out_specs=(pl.BlockSpec(tok, fwd3), pl.BlockSpec(tok, bwd3)),
        out_shape=(jax.ShapeDtypeStruct((bsz, seqlen, A_WIDTH), BF16),
                   jax.ShapeDtypeStruct((bsz, seqlen, A_WIDTH), BF16)),
        scratch_shapes=[pltpu.VMEM((bsz * 2 * A_HEADS, A_HEAD_DIM, 2 * A_HEAD_DIM), F32),
                        pltpu.VMEM((bsz * 2, N_CHAN, LANES), F32)],
        compiler_params=_params("arbitrary"),
        name="mlstm",
    )(q, q, kt, kt, va, va, ar, ar, br, br, cc, cc, bc, bc)


def _t5_bucket(rel):
    nb = N_BUCKETS // 2
    max_exact = nb // 2
    ret = jnp.where(rel > 0, nb, 0)
    n = jnp.abs(rel)
    nf = jnp.maximum(n, 1).astype(jnp.float32)
    large = max_exact + (jnp.log(nf / max_exact) / math.log(MAX_DISTANCE / max_exact)
                         * (nb - max_exact)).astype(jnp.int32)
    large = jnp.minimum(large, nb - 1)
    return ret + jnp.where(n < max_exact, n, large)


def _bias_kernel(rb_ref, bucket_ref, bias_ref):
    bucket = bucket_ref[...]
    qi = lax.broadcasted_iota(jnp.int32, bucket.shape, 0)
    kj = lax.broadcasted_iota(jnp.int32, bucket.shape, 1)
    band = jnp.abs(kj - BLOCK - qi) <= WINDOW
    masks = (band & (kj >= BLOCK), band, band & (kj < 2 * BLOCK))
    for hq in range(B_Q_HEADS):
        acc = jnp.zeros(bucket.shape, F32)
        for nb in range(N_BUCKETS):
            acc = jnp.where(bucket == nb, rb_ref[nb, hq], acc)
        acc = acc * LOG2E
        for v, mask in enumerate(masks):
            bias_ref[v, hq] = jnp.where(mask, acc, NEG_INF)


def _bias_table(rel_bias):
    q_off = jnp.arange(BLOCK)
    k_off = jnp.arange(3 * BLOCK) - BLOCK
    bucket = _t5_bucket(k_off[None, :] - q_off[:, None]).astype(jnp.int32)
    shape = (3, B_Q_HEADS, BLOCK, 3 * BLOCK)
    return pl.pallas_call(
        _bias_kernel,
        in_specs=[pl.BlockSpec(memory_space=pltpu.SMEM),
                  pl.BlockSpec(bucket.shape, lambda: (0, 0))],
        out_specs=pl.BlockSpec(shape, lambda: (0, 0, 0, 0)),
        out_shape=jax.ShapeDtypeStruct(shape, F32),
        name="bias_table",
    )(rel_bias.astype(F32), bucket)


def _attn_out_kernel(sink_ref, q_ref, kp_ref, kc_ref, kn_ref, vp_ref, vc_ref, vn_ref, zb_ref,
                     bias_ref, hf_ref, hb_ref, og_ref, zg_ref, x_ref, mw_ref, wo_ref, fw_ref,
                     o_ref, s_scr, yb_scr, *, n_qb, layer, final):
    j = pl.program_id(1)
    last = pl.num_programs(1) - 1
    n_keys = (n_qb + 2) * BLOCK
    lo_k = lax.broadcasted_iota(jnp.int32, (n_keys, LANES), 1) < B_HEAD_DIM
    lo_q = lax.broadcasted_iota(jnp.int32, (BLOCK, LANES), 1) < B_HEAD_DIM
    zero = jnp.zeros((n_keys, LANES), BF16)
    k_sel, v_sel = [], []
    for h in range(B_KV_HEADS):
        hs = slice(h * LANES, (h + 1) * LANES)
        k2 = jnp.concatenate([kp_ref[0, :, hs], kc_ref[0, :, hs], kn_ref[0, :, hs]], axis=0)
        v2 = jnp.concatenate([vp_ref[0, :, hs], vc_ref[0, :, hs], vn_ref[0, :, hs]], axis=0)
        k_sel.append((jnp.where(lo_k, k2, zero), jnp.where(lo_k, zero, k2)))
        v_sel.append((jnp.where(lo_k, v2, zero), jnp.where(lo_k, zero, v2)))

    for i in range(n_qb):
        variant = jnp.int32(1)
        if i == 0:
            variant = jnp.where(j == 0, 0, variant)
        if i == n_qb - 1:
            variant = jnp.where(j == last, 2, variant)
        rows = slice(i * BLOCK, (i + 1) * BLOCK)
        win = slice(i * BLOCK, (i + 3) * BLOCK)
        for p_idx in range(B_Q_HEADS // 2):
            qp = q_ref[0, rows, p_idx * LANES:(p_idx + 1) * LANES]
            for par in range(2):
                hq = p_idx * 2 + par
                k_win = k_sel[p_idx // 2][par][win]
                s_scr[i * B_Q_HEADS + hq] = _dot_nt(qp, k_win) + bias_ref[variant, hq]

    for i in range(n_qb):
        rows = slice(i * BLOCK, (i + 1) * BLOCK)
        win = slice(i * BLOCK, (i + 3) * BLOCK)
        for p_idx in range(B_Q_HEADS // 2):
            ps = slice(p_idx * LANES, (p_idx + 1) * LANES)
            probs, dens = [], []
            for par in range(2):
                hq = p_idx * 2 + par
                sink = sink_ref[layer, hq] * LOG2E
                sc = s_scr[i * B_Q_HEADS + hq]
                m = jnp.maximum(jnp.max(sc, axis=-1, keepdims=True), sink)
                p = jnp.exp2(sc - m)
                dens.append(jnp.sum(p, axis=-1, keepdims=True) + jnp.exp2(sink - m))
                probs.append(p.astype(BF16))
            v_even, v_odd = v_sel[p_idx // 2]
            v_bd = jnp.concatenate([v_even[win], v_odd[win]], axis=0)
            out = _dot(jnp.concatenate(probs, axis=1), v_bd)
            y = out / jnp.where(lo_q, dens[0], dens[1])
            yb_scr[rows, ps] = (y * zb_ref[0, rows, ps].astype(F32)).astype(BF16)

    h = og_ref[0].astype(F32) * (hf_ref[0].astype(F32) + hb_ref[0].astype(F32))
    parts = []
    for k in range(A_HEADS):
        hs = slice(k * A_HEAD_DIM, (k + 1) * A_HEAD_DIM)
        hh = h[:, hs]
        hh = hh * lax.rsqrt(jnp.mean(hh * hh, axis=-1, keepdims=True) + EPS)
        parts.append(hh * mw_ref[0, :, hs])
    ya = (jnp.concatenate(parts, axis=1) * zg_ref[0].astype(F32)).astype(BF16)
    out = x_ref[0] + _dot(ya, wo_ref[0, :A_WIDTH, :]) + _dot(yb_scr[...], wo_ref[0, A_WIDTH:, :])
    if final:
        out = out * lax.rsqrt(jnp.mean(out * out, axis=-1, keepdims=True) + EPS) * fw_ref[...]
    o_ref[0] = out


def _attn_out(sink, qb, kb2, vb2, zb, bias, hf, hb, og, zg, x, mw, wo, fw, *, n_qb, layer, final):
    bsz, seqlen, _ = qb.shape
    nb = seqlen // BLOCK
    assert nb >= 2 and nb % n_qb == 0
    tq = n_qb * BLOCK
    cur = lambda b, j: (b, j, 0)
    prev = lambda b, j: (b, jnp.maximum(j * n_qb - 1, 0), 0)
    nxt = lambda b, j: (b, jnp.minimum((j + 1) * n_qb, nb - 1), 0)
    halo = pl.BlockSpec((1, BLOCK, 2 * LANES), prev), pl.BlockSpec((1, BLOCK, 2 * LANES), nxt)
    kv_cur = pl.BlockSpec((1, tq, 2 * LANES), cur)

    def tok(width):
        return pl.BlockSpec((1, tq, width), cur)

    def full(arr):
        return pl.BlockSpec(arr.shape, lambda b, j: (0,) * arr.ndim)

    return pl.pallas_call(
        functools.partial(_attn_out_kernel, n_qb=n_qb, layer=layer, final=final),
        grid=(bsz, nb // n_qb),
        in_specs=[pl.BlockSpec(memory_space=pltpu.SMEM),
                  tok(B_WIDTH),
                  halo[0], kv_cur, halo[1],
                  halo[0], kv_cur, halo[1],
                  tok(B_WIDTH), full(bias),
                  tok(A_WIDTH), tok(A_WIDTH), tok(A_WIDTH), tok(A_WIDTH), tok(D_MODEL),
                  _layer_spec(mw, layer), _layer_spec(wo, layer), full(fw)],
        out_specs=tok(D_MODEL),
        out_shape=jax.ShapeDtypeStruct(x.shape, F32),
        scratch_shapes=[pltpu.VMEM((n_qb * B_Q_HEADS, BLOCK, 3 * BLOCK), F32),
                        pltpu.VMEM((tq, B_WIDTH), BF16)],
        compiler_params=_params("parallel", "parallel"),
        name="attn_out",
    )(sink, qb, kb2, kb2, kb2, vb2, vb2, vb2, zb, bias, hf, hb, og, zg, x, mw, wo, fw)


def _tile(n, target):
    t = min(n, target)
    assert n % t == 0, (n, t)
    return t


def kernel(x, norm_w, w_in, conv_w, conv_b, gate_b, mhn_w, sink, rel_bias, w_out, final_norm_w):
    bsz, seqlen, d_model = x.shape
    depth = norm_w.shape[0]
    assert d_model == D_MODEL and seqlen % CHUNK == 0
    tm = _tile(seqlen, 1024)
    n_qb = _tile(seqlen // BLOCK, 4)

    c_qk = 2 * A_WIDTH
    c_a = c_qk + 3 * A_WIDTH
    c_g = c_a + N_GATES
    assert c_a == 5 * A_WIDTH
    w_bf = w_in.astype(BF16)
    wgt = jnp.swapaxes(w_bf[:, :, c_a:c_g], 1, 2)
    wb = w_bf[:, :, c_g:]
    wo = w_out.astype(BF16)
    cw = jnp.pad(conv_w, ((0, 0), (0, SUBLANES - CONV_K), (0, 0)))
    nw = norm_w.reshape(depth, 1, D_MODEL)
    cb = conv_b.reshape(depth, 1, 2 * A_WIDTH)
    gb = gate_b.reshape(depth, N_GATES, 1)
    mw = mhn_w.reshape(depth, 1, A_WIDTH)
    bias = _bias_table(rel_bias)
    fw = final_norm_w.reshape(1, D_MODEL)

    xf = x
    for l in range(depth):
        q, kt, va, og, zg, qb, kb2, vb2, zb, gr = _in_proj(
            xf, nw, w_bf, wgt, wb, cw, cb, gb, tm=tm, layer=l)
        ar, br, cc, bc = _gate_prep(gr)
        hf, hb = _mlstm(q, kt, va, ar, br, cc, bc)
        xf = _attn_out(sink, qb, kb2, vb2, zb, bias, hf, hb, og, zg, xf, mw, wo, fw,
                       n_qb=n_qb, layer=l, final=(l == depth - 1))
    return xf
```

```python
import functools
import math

import jax
import jax.numpy as jnp
import numpy as np
from jax import lax
from jax.experimental import pallas as pl
from jax.experimental.pallas import tpu as pltpu

D_MODEL = 1024
A_WIDTH = 512
A_HEADS = 4
A_HEAD_DIM = 128
CHUNK = 128
CONV_K = 5
B_WIDTH = 512
B_HEAD_DIM = 64
B_Q_HEADS = 8
B_KV_HEADS = 2
WINDOW = 128
BLOCK = 128
N_BUCKETS = 32
MAX_DISTANCE = 128
EPS = 1e-6
NEG_INF = -1e30
LOG2E = math.log2(math.e)
N_GATES = 4 * A_HEADS
N_CHAN = 2 * A_HEADS

LANES = 128
SUBLANES = 8
VMEM_LIMIT_BYTES = 56 * 1024 * 1024

HALO = 2 * SUBLANES
BF16 = jnp.bfloat16
F32 = jnp.float32


def _params(*sem):
    return pltpu.CompilerParams(dimension_semantics=sem, vmem_limit_bytes=VMEM_LIMIT_BYTES)


def _dot(a, b):
    return jnp.dot(a, b, preferred_element_type=F32)


def _dot_nt(a, b):
    return lax.dot_general(a, b, (((1,), (1,)), ((), ())), preferred_element_type=F32)


def _log_sigmoid(x):
    return -(jnp.maximum(-x, 0.0) + jnp.log1p(jnp.exp(-jnp.abs(x))))


def _sigmoid(x):
    return 0.5 * jnp.tanh(0.5 * x) + 0.5


def _silu_of_half(h):
    return h + h * jnp.tanh(h)


def _layer_spec(arr, layer):
    return pl.BlockSpec((1,) + arr.shape[1:], lambda *_: (layer,) + (0,) * (arr.ndim - 1))


def _in_proj_kernel(x_ref, xp_ref, xn_ref, nw_ref, wqk_ref, wv_ref, wo_ref, wz_ref, wgt_ref, wb_ref,
                    cw_ref, cb_ref, gb_ref,
                    q_ref, kt_ref, va_ref, og_ref, zg_ref, qb_ref, kb2_ref, vb2_ref, zb_ref,
                    ar_ref, br_ref, cpk_ref, bpk_ref, u_scr, *, tm):
    i = pl.program_id(1)
    last = pl.num_programs(1) - 1
    nw = nw_ref[0]

    def norm(xv):
        y = xv * lax.rsqrt(jnp.mean(xv * xv, axis=-1, keepdims=True) + EPS)
        return (y * nw).astype(BF16)

    hn = norm(x_ref[0])
    g = _dot_nt(wgt_ref[0], hn) + gb_ref[0]
    row = lax.broadcasted_iota(jnp.int32, g.shape, 0)
    _gate_scan(jnp.where(row < N_CHAN, g, _log_sigmoid(g)), ar_ref, br_ref, cpk_ref, bpk_ref, tm)
    hp = norm(xp_ref[0])
    hx = norm(xn_ref[0])
    u_all = _dot(jnp.concatenate([hp, hn, hx], axis=0), wqk_ref[0])
    u_top = jnp.where(i == 0, 0.0, u_all[:HALO])
    u_bot = jnp.where(i == last, 0.0, u_all[HALO + tm:])
    cw_half = 0.5 * cw_ref[0]
    cb_half = 0.5 * cb_ref[0]
    parts = []
    pad = CONV_K // 2
    for c in range(2 * A_WIDTH // LANES):
        cs = slice(c * LANES, (c + 1) * LANES)
        u_scr[c, :HALO, :] = u_top[:, cs]
        u_scr[c, HALO:HALO + tm, :] = u_all[HALO:HALO + tm, cs]
        u_scr[c, HALO + tm:, :] = u_bot[:, cs]
        acc = cb_half[:, cs]
        for tap in range(CONV_K):
            lo = HALO - pad + tap
            acc = acc + u_scr[c, lo:lo + tm, :] * cw_half[tap:tap + 1, cs]
        parts.append(acc)
    qk = _silu_of_half(jnp.concatenate(parts, axis=1))
    q_ref[0] = qk[:, :A_WIDTH].astype(BF16)
    k = qk[:, A_WIDTH:] * (A_HEAD_DIM ** -0.5)
    kt_ref[0] = k.T.astype(BF16)

    va_ref[0] = _dot(hn, wv_ref[0]).astype(BF16)
    og_ref[0] = _sigmoid(_dot(hn, wo_ref[0])).astype(BF16)
    zg_ref[0] = _silu_of_half(0.5 * _dot(hn, wz_ref[0])).astype(BF16)

    bq = _dot(hn, wb_ref[0, :, :B_WIDTH])
    qb_ref[0] = (bq * (B_HEAD_DIM ** -0.5 * LOG2E)).astype(BF16)
    kv = _dot(hn, wb_ref[0, :, B_WIDTH:B_WIDTH + 2 * LANES])
    half = lax.broadcasted_iota(jnp.int32, (tm, LANES), 1) < B_HEAD_DIM
    for src, dst in ((kv[:, :LANES], kb2_ref), (kv[:, LANES:], vb2_ref)):
        sw = pltpu.roll(src, B_HEAD_DIM, 1)
        dst[0, :, :LANES] = jnp.where(half, src, sw).astype(BF16)
        dst[0, :, LANES:] = jnp.where(half, sw, src).astype(BF16)
    zb_ref[0] = _silu_of_half(0.5 * _dot(hn, wb_ref[0, :, B_WIDTH + 2 * LANES:])).astype(BF16)


def _in_proj(x, nw, w_all, wgt, wb, cw, cb, gb, *, tm, layer):
    bsz, seqlen, _ = x.shape
    nt = seqlen // tm
    hb = tm // HALO
    nhb = seqlen // HALO

    def full(arr):
        return _layer_spec(arr, layer)

    def w_cols(start, width):
        assert start % width == 0 and width % LANES == 0
        return pl.BlockSpec((1, D_MODEL, width), lambda b, i: (layer, 0, start // width))

    def rows(width):
        return pl.BlockSpec((1, tm, width), lambda b, i: (b, i, 0))

    out_shape = (
        jax.ShapeDtypeStruct((bsz, seqlen, A_WIDTH), BF16),
        jax.ShapeDtypeStruct((bsz, A_WIDTH, seqlen), BF16),
        jax.ShapeDtypeStruct((bsz, seqlen, A_WIDTH), BF16),
        jax.ShapeDtypeStruct((bsz, seqlen, A_WIDTH), BF16),
        jax.ShapeDtypeStruct((bsz, seqlen, A_WIDTH), BF16),
        jax.ShapeDtypeStruct((bsz, seqlen, B_WIDTH), BF16),
        jax.ShapeDtypeStruct((bsz, seqlen, 2 * LANES), BF16),
        jax.ShapeDtypeStruct((bsz, seqlen, 2 * LANES), BF16),
        jax.ShapeDtypeStruct((bsz, seqlen, B_WIDTH), BF16),
        jax.ShapeDtypeStruct((bsz, N_CHAN, seqlen), F32),
        jax.ShapeDtypeStruct((bsz, N_CHAN, seqlen), F32),
        jax.ShapeDtypeStruct((bsz, nt, CHUNK, LANES), F32),
        jax.ShapeDtypeStruct((bsz, nt, CHUNK, LANES), F32),
    )
    out_specs = (
        rows(A_WIDTH),
        pl.BlockSpec((1, A_WIDTH, tm), lambda b, i: (b, 0, i)),
        rows(A_WIDTH), rows(A_WIDTH), rows(A_WIDTH), rows(B_WIDTH),
        rows(2 * LANES), rows(2 * LANES), rows(B_WIDTH),
        pl.BlockSpec((1, N_CHAN, tm), lambda b, i: (b, 0, i)),
        pl.BlockSpec((1, N_CHAN, tm), lambda b, i: (b, 0, i)),
        pl.BlockSpec((1, 1, CHUNK, LANES), lambda b, i: (b, i, 0, 0)),
        pl.BlockSpec((1, 1, CHUNK, LANES), lambda b, i: (b, i, 0, 0)),
    )
    in_specs = [
        rows(D_MODEL),
        pl.BlockSpec((1, HALO, D_MODEL), lambda b, i: (b, jnp.maximum(i * hb - 1, 0), 0)),
        pl.BlockSpec((1, HALO, D_MODEL), lambda b, i: (b, jnp.minimum((i + 1) * hb, nhb - 1), 0)),
        full(nw),
        w_cols(0, 2 * A_WIDTH),
        w_cols(2 * A_WIDTH, A_WIDTH),
        w_cols(3 * A_WIDTH, A_WIDTH),
        w_cols(4 * A_WIDTH, A_WIDTH),
        full(wgt), full(wb), full(cw), full(cb), full(gb),
    ]
    return pl.pallas_call(
        functools.partial(_in_proj_kernel, tm=tm),
        grid=(bsz, nt),
        in_specs=in_specs,
        out_specs=out_specs,
        out_shape=out_shape,
        scratch_shapes=[pltpu.VMEM((2 * A_WIDTH // LANES, tm + 2 * HALO, LANES), F32)],
        compiler_params=_params("parallel", "arbitrary"),
        name="in_proj",
    )(x, x, x, nw, w_all, w_all, w_all, w_all, wgt, wb, cw, cb, gb)


GROUP_CHUNKS = 8


def _gate_scan(gates, ar_ref, br_ref, cpk_ref, bpk_ref, tm):
    n_chunks = tm // CHUNK
    n_rows = n_chunks * N_CHAN
    ti = lax.broadcasted_iota(jnp.int32, (CHUNK, CHUNK), 0)
    si = lax.broadcasted_iota(jnp.int32, (CHUNK, CHUNK), 1)
    upper_f = (ti <= si).astype(F32)
    lower_f = (ti >= si).astype(F32)
    ri = lax.broadcasted_iota(jnp.int32, (n_rows, CHUNK), 0)
    lane = lax.broadcasted_iota(jnp.int32, (n_rows, CHUNK), 1)
    fwd_row = lax.rem(ri, N_CHAN) < A_HEADS
    li = jnp.concatenate([gates[:N_CHAN, c * CHUNK:(c + 1) * CHUNK]
                          for c in range(n_chunks)], axis=0) * LOG2E
    lf = jnp.concatenate([gates[N_CHAN:, c * CHUNK:(c + 1) * CHUNK]
                          for c in range(n_chunks)], axis=0) * LOG2E
    b_pre = jnp.dot(lf, upper_f, preferred_element_type=F32, precision=lax.Precision.HIGHEST)
    b_suf = jnp.dot(lf, lower_f, preferred_element_type=F32, precision=lax.Precision.HIGHEST)
    b = jnp.where(fwd_row, b_pre, b_suf)
    a = li - b
    pre, suf = a, a
    sh = 1
    while sh < CHUNK:
        pre = jnp.where(lane >= sh, jnp.maximum(pre, pltpu.roll(pre, sh, 1)), pre)
        suf = jnp.where(lane < CHUNK - sh, jnp.maximum(suf, pltpu.roll(suf, CHUNK - sh, 1)), suf)
        sh *= 2
    cmax = jnp.where(fwd_row, pre, suf)
    for c in range(n_chunks):
        rs = slice(c * N_CHAN, (c + 1) * N_CHAN)
        ar_ref[0, :, c * CHUNK:(c + 1) * CHUNK] = a[rs, :]
        br_ref[0, :, c * CHUNK:(c + 1) * CHUNK] = b[rs, :]
    zpad = jnp.zeros((LANES - n_rows, CHUNK), F32)
    cpk_ref[0, 0] = jnp.concatenate([cmax, zpad], axis=0).T
    bpk_ref[0, 0] = jnp.concatenate([b, zpad], axis=0).T


CHUNKS_PER_STEP = 4


def _mlstm_kernel(qf_ref, qb_ref, ktf_ref, ktb_ref, vf_ref, vb_ref,
                  arf_ref, arb_ref, brf_ref, brb_ref, ccf_ref, ccb_ref, bcf_ref, bcb_ref,
                  hf_ref, hb_ref, c_scr, m_scr, *, bsz):
    j = pl.program_id(0)

    @pl.when(j == 0)
    def _():
        c_scr[...] = jnp.zeros_like(c_scr)
        m_scr[...] = jnp.zeros_like(m_scr)

    ti = lax.broadcasted_iota(jnp.int32, (CHUNK, CHUNK), 0)
    si = lax.broadcasted_iota(jnp.int32, (CHUNK, CHUNK), 1)
    ones_blk = jnp.ones((CHUNK, A_HEAD_DIM), BF16)
    sub8 = lax.broadcasted_iota(jnp.int32, (N_CHAN, LANES), 0)
    lane8 = lax.broadcasted_iota(jnp.int32, (N_CHAN, LANES), 1)

    n_steps = pl.num_programs(0)
    for k in range(CHUNKS_PER_STEP):
        _mlstm_chunk(j, k, n_steps, qf_ref, qb_ref, ktf_ref, ktb_ref, vf_ref, vb_ref,
                     arf_ref, arb_ref, brf_ref, brb_ref, ccf_ref, ccb_ref, bcf_ref, bcb_ref,
                     hf_ref, hb_ref, c_scr, m_scr, bsz, ti, si, ones_blk, sub8, lane8)


def _mlstm_chunk(j, k, n_steps, qf_ref, qb_ref, ktf_ref, ktb_ref, vf_ref, vb_ref,
                 arf_ref, arb_ref, brf_ref, brb_ref, ccf_ref, ccb_ref, bcf_ref, bcb_ref,
                 hf_ref, hb_ref, c_scr, m_scr, bsz, ti, si, ones_blk, sub8, lane8):
    jf = j * CHUNKS_PER_STEP + k
    jb = n_steps * CHUNKS_PER_STEP - 1 - jf
    base_f = lax.rem(jf, GROUP_CHUNKS) * N_CHAN
    base_b = lax.rem(jb, GROUP_CHUNKS) * N_CHAN
    rows_f = slice(k * CHUNK, (k + 1) * CHUNK)
    rows_b = slice((CHUNKS_PER_STEP - 1 - k) * CHUNK, (CHUNKS_PER_STEP - k) * CHUNK)
    dirs = (
        (0, qf_ref, ktf_ref, vf_ref, arf_ref, brf_ref, ccf_ref, bcf_ref, hf_ref, ti >= si, CHUNK - 1,
         base_f, rows_f),
        (1, qb_ref, ktb_ref, vb_ref, arb_ref, brb_ref, ccb_ref, bcb_ref, hb_ref, ti <= si, 0,
         base_b, rows_b),
    )

    def body(b, carry):
        tiles = []
        for d, q_ref, kt_ref, v_ref, ar_ref, br_ref, cc_ref, bc_ref, h_ref, vis, last, base, rows in dirs:
            for h in range(A_HEADS):
                hs = slice(h * A_HEAD_DIM, (h + 1) * A_HEAD_DIM)
                idx = (b * 2 + d) * A_HEADS + h
                q = q_ref[b, rows, hs]
                kt = kt_ref[b, hs, rows]
                tiles.append((_dot(q, kt), q, kt, idx, hs))

        gates = []
        for d, q_ref, kt_ref, v_ref, ar_ref, br_ref, cc_ref, bc_ref, h_ref, vis, last, base, rows in dirs:
            a_row = ar_ref[b, :, rows]
            b_last = br_ref[b, :, rows][:, last:last + 1]
            m8 = m_scr[b * 2 + d]
            gl8 = jnp.maximum(m8, jnp.max(a_row, axis=1, keepdims=True))
            w_row = jnp.exp2(a_row - gl8)
            decay8 = jnp.exp2(m8 - gl8)
            m_scr[b * 2 + d] = b_last + gl8
            m_lane = jnp.sum(jnp.where(sub8 == lane8, m8, 0.0), axis=0, keepdims=True)
            unrot = lax.rem(LANES - base, LANES)
            g = jnp.maximum(m_lane, pltpu.roll(cc_ref[b, 0], unrot, 1))
            emt = jnp.exp2(-(pltpu.roll(bc_ref[b, 0], unrot, 1) + g))
            gates.append((a_row, w_row, decay8, m8, g, emt))

        for d, q_ref, kt_ref, v_ref, ar_ref, br_ref, cc_ref, bc_ref, h_ref, vis, last, base, rows in dirs:
            a_row, w_row, decay8, m8, g, emt = gates[d]
            for h in range(A_HEADS):
                ch = d * A_HEADS + h
                qk, q, kt, idx, hs = tiles[ch]
                c_old = c_scr[idx]
                v_aug = jnp.concatenate([v_ref[b, rows, hs], ones_blk], axis=1)
                g_b = jnp.broadcast_to(g[:, ch:ch + 1], (CHUNK, LANES))
                p = jnp.where(vis, jnp.exp2(a_row[ch:ch + 1, :] - g_b), 0.0)
                iw_b = jnp.exp2(m8[ch:ch + 1, :] - g_b)
                lhs = jnp.concatenate([(qk * p).astype(BF16),
                                       (q.astype(F32) * iw_b).astype(BF16)], axis=1)
                rhs = jnp.concatenate([v_aug, c_old.astype(BF16)], axis=0)
                r = _dot(lhs, rhs)
                num = r[:, :A_HEAD_DIM]
                den = r[:, A_HEAD_DIM:]
                h_ref[b, rows, hs] = (num / jnp.maximum(jnp.abs(den), emt[:, ch:ch + 1])).astype(BF16)
                ktw = (kt.astype(F32) * w_row[ch:ch + 1, :]).astype(BF16)
                c_scr[idx] = decay8[ch:ch + 1, :1] * c_old + _dot(ktw, v_aug)
        return carry

    lax.fori_loop(0, bsz, body, 0, unroll=True)


def _mlstm(q, kt, va, ar, br, cc, bc):
    bsz, seqlen, _ = q.shape
    nc = seqlen // CHUNK
    cps = CHUNKS_PER_STEP
    assert nc % cps == 0 and GROUP_CHUNKS % cps == 0
    ns = nc // cps
    fwd3 = lambda j: (0, j, 0)
    bwd3 = lambda j: (0, ns - 1 - j, 0)
    fwd3t = lambda j: (0, 0, j)
    bwd3t = lambda j: (0, 0, ns - 1 - j)
    tok = (bsz, cps * CHUNK, A_WIDTH)
    tok_t = (bsz, A_WIDTH, cps * CHUNK)
    rowb = (bsz, N_CHAN, cps * CHUNK)
    colb = (bsz, 1, CHUNK, LANES)
    fwd4 = lambda j: (0, (j * cps) // GROUP_CHUNKS, 0, 0)
    bwd4 = lambda j: (0, (nc - 1 - j * cps) // GROUP_CHUNKS, 0, 0)
    in_specs = [
        pl.BlockSpec(tok, fwd3), pl.BlockSpec(tok, bwd3),
        pl.BlockSpec(tok_t, fwd3t), pl.BlockSpec(tok_t, bwd3t),
        pl.BlockSpec(tok, fwd3), pl.BlockSpec(tok, bwd3),
        pl.BlockSpec(rowb, fwd3t), pl.BlockSpec(rowb, bwd3t),
        pl.BlockSpec(rowb, fwd3t), pl.BlockSpec(rowb, bwd3t),
        pl.BlockSpec(colb, fwd4), pl.BlockSpec(colb, bwd4),
        pl.BlockSpec(colb, fwd4), pl.BlockSpec(colb, bwd4),
    ]
    return pl.pallas_call(
        functools.partial(_mlstm_kernel, bsz=bsz),
        grid=(ns,),
        in_specs=in_specs,
        out_specs=(pl.BlockSpec(tok, fwd3), pl.BlockSpec(tok, bwd3)),
        out_shape=(jax.ShapeDtypeStruct((bsz, seqlen, A_WIDTH), BF16),
                   jax.ShapeDtypeStruct((bsz, seqlen, A_WIDTH), BF16)),
        scratch_shapes=[pltpu.VMEM((bsz * 2 * A_HEADS, A_HEAD_DIM, 2 * A_HEAD_DIM), F32),
                        pltpu.VMEM((bsz * 2, N_CHAN, LANES), F32)],
        compiler_params=_params("arbitrary"),
        name="mlstm",
    )(q, q, kt, kt, va, va, ar, ar, br, br, cc, cc, bc, bc)


def _t5_bucket(rel):
    nb = N_BUCKETS // 2
    max_exact = nb // 2
    ret = jnp.where(rel > 0, nb, 0)
    n = jnp.abs(rel)
    nf = jnp.maximum(n, 1).astype(jnp.float32)
    large = max_exact + (jnp.log(nf / max_exact) / math.log(MAX_DISTANCE / max_exact)
                         * (nb - max_exact)).astype(jnp.int32)
    large = jnp.minimum(large, nb - 1)
    return ret + jnp.where(n < max_exact, n, large)


def _bias_kernel(rb_ref, bucket_ref, bias_ref):
    bucket = bucket_ref[...]
    qi = lax.broadcasted_iota(jnp.int32, bucket.shape, 0)
    kj = lax.broadcasted_iota(jnp.int32, bucket.shape, 1)
    band = jnp.abs(kj - BLOCK - qi) <= WINDOW
    masks = (band & (kj >= BLOCK), band, band & (kj < 2 * BLOCK))
    for hq in range(B_Q_HEADS):
        acc = jnp.zeros(bucket.shape, F32)
        for nb in range(N_BUCKETS):
            acc = jnp.where(bucket == nb, rb_ref[nb, hq], acc)
        acc = acc * LOG2E
        for v, mask in enumerate(masks):
            bias_ref[v, hq] = jnp.where(mask, acc, NEG_INF)


def _bias_table(rel_bias):
    q_off = jnp.arange(BLOCK)
    k_off = jnp.arange(3 * BLOCK) - BLOCK
    bucket = _t5_bucket(k_off[None, :] - q_off[:, None]).astype(jnp.int32)
    shape = (3, B_Q_HEADS, BLOCK, 3 * BLOCK)
    return pl.pallas_call(
        _bias_kernel,
        in_specs=[pl.BlockSpec(memory_space=pltpu.SMEM),
                  pl.BlockSpec(bucket.shape, lambda: (0, 0))],
        out_specs=pl.BlockSpec(shape, lambda: (0, 0, 0, 0)),
        out_shape=jax.ShapeDtypeStruct(shape, F32),
        name="bias_table",
    )(rel_bias.astype(F32), bucket)


def _attn_out_kernel(sink_ref, q_ref, kp_ref, kc_ref, kn_ref, vp_ref, vc_ref, vn_ref, zb_ref,
                     bias_ref, hf_ref, hb_ref, og_ref, zg_ref, x_ref, mw_ref, wo_ref, fw_ref,
                     o_ref, s_scr, yb_scr, *, n_qb, layer, final):
    j = pl.program_id(1)
    last = pl.num_programs(1) - 1
    n_keys = (n_qb + 2) * BLOCK
    lo_k = lax.broadcasted_iota(jnp.int32, (n_keys, LANES), 1) < B_HEAD_DIM
    lo_q = lax.broadcasted_iota(jnp.int32, (BLOCK, LANES), 1) < B_HEAD_DIM
    zero = jnp.zeros((n_keys, LANES), BF16)
    k_sel, v_sel = [], []
    for h in range(B_KV_HEADS):
        hs = slice(h * LANES, (h + 1) * LANES)
        k2 = jnp.concatenate([kp_ref[0, :, hs], kc_ref[0, :, hs], kn_ref[0, :, hs]], axis=0)
        v2 = jnp.concatenate([vp_ref[0, :, hs], vc_ref[0, :, hs], vn_ref[0, :, hs]], axis=0)
        k_sel.append((jnp.where(lo_k, k2, zero), jnp.where(lo_k, zero, k2)))
        v_sel.append((jnp.where(lo_k, v2, zero), jnp.where(lo_k, zero, v2)))

    for i in range(n_qb):
        variant = jnp.int32(1)
        if i == 0:
            variant = jnp.where(j == 0, 0, variant)
        if i == n_qb - 1:
            variant = jnp.where(j == last, 2, variant)
        rows = slice(i * BLOCK, (i + 1) * BLOCK)
        win = slice(i * BLOCK, (i + 3) * BLOCK)
        for p_idx in range(B_Q_HEADS // 2):
            qp = q_ref[0, rows, p_idx * LANES:(p_idx + 1) * LANES]
            for par in range(2):
                hq = p_idx * 2 + par
                k_win = k_sel[p_idx // 2][par][win]
                s_scr[i * B_Q_HEADS + hq] = _dot_nt(qp, k_win) + bias_ref[variant, hq]

    for i in range(n_qb):
        rows = slice(i * BLOCK, (i + 1) * BLOCK)
        win = slice(i * BLOCK, (i + 3) * BLOCK)
        for p_idx in range(B_Q_HEADS // 2):
            ps = slice(p_idx * LANES, (p_idx + 1) * LANES)
            probs, dens = [], []
            for par in range(2):
                hq = p_idx * 2 + par
                sink = sink_ref[layer, hq] * LOG2E
                sc = s_scr[i * B_Q_HEADS + hq]
                m = jnp.maximum(jnp.max(sc, axis=-1, keepdims=True), sink)
                p = jnp.exp2(sc - m)
                dens.append(jnp.sum(p, axis=-1, keepdims=True) + jnp.exp2(sink - m))
                probs.append(p.astype(BF16))
            v_even, v_odd = v_sel[p_idx // 2]
            v_bd = jnp.concatenate([v_even[win], v_odd[win]], axis=0)
            out = _dot(jnp.concatenate(probs, axis=1), v_bd)
            y = out / jnp.where(lo_q, dens[0], dens[1])
            yb_scr[rows, ps] = (y * zb_ref[0, rows, ps].astype(F32)).astype(BF16)

    h = og_ref[0].astype(F32) * (hf_ref[0].astype(F32) + hb_ref[0].astype(F32))
    parts = []
    for k in range(A_HEADS):
        hs = slice(k * A_HEAD_DIM, (k + 1) * A_HEAD_DIM)
        hh = h[:, hs]
        hh = hh * lax.rsqrt(jnp.mean(hh * hh, axis=-1, keepdims=True) + EPS)
        parts.append(hh * mw_ref[0, :, hs])
    ya = (jnp.concatenate(parts, axis=1) * zg_ref[0].astype(F32)).astype(BF16)
    out = x_ref[0] + _dot(ya, wo_ref[0, :A_WIDTH, :]) + _dot(yb_scr[...], wo_ref[0, A_WIDTH:, :])
    if final:
        out = out * lax.rsqrt(jnp.mean(out * out, axis=-1, keepdims=True) + EPS) * fw_ref[...]
    o_ref[0] = out


def _attn_out(sink, qb, kb2, vb2, zb, bias, hf, hb, og, zg, x, mw, wo, fw, *, n_qb, layer, final):
    bsz, seqlen, _ = qb.shape
    nb = seqlen // BLOCK
    assert nb >= 2 and nb % n_qb == 0
    tq = n_qb * BLOCK
    cur = lambda b, j: (b, j, 0)
    prev = lambda b, j: (b, jnp.maximum(j * n_qb - 1, 0), 0)
    nxt = lambda b, j: (b, jnp.minimum((j + 1) * n_qb, nb - 1), 0)
    halo = pl.BlockSpec((1, BLOCK, 2 * LANES), prev), pl.BlockSpec((1, BLOCK, 2 * LANES), nxt)
    kv_cur = pl.BlockSpec((1, tq, 2 * LANES), cur)

    def tok(width):
        return pl.BlockSpec((1, tq, width), cur)

    def full(arr):
        return pl.BlockSpec(arr.shape, lambda b, j: (0,) * arr.ndim)

    return pl.pallas_call(
        functools.partial(_attn_out_kernel, n_qb=n_qb, layer=layer, final=final),
        grid=(bsz, nb // n_qb),
        in_specs=[pl.BlockSpec(memory_space=pltpu.SMEM),
                  tok(B_WIDTH),
                  halo[0], kv_cur, halo[1],
                  halo[0], kv_cur, halo[1],
                  tok(B_WIDTH), full(bias),
                  tok(A_WIDTH), tok(A_WIDTH), tok(A_WIDTH), tok(A_WIDTH), tok(D_MODEL),
                  _layer_spec(mw, layer), _layer_spec(wo, layer), full(fw)],
        out_specs=tok(D_MODEL),
        out_shape=jax.ShapeDtypeStruct(x.shape, F32),
        scratch_shapes=[pltpu.VMEM((n_qb * B_Q_HEADS, BLOCK, 3 * BLOCK), F32),
                        pltpu.VMEM((tq, B_WIDTH), BF16)],
        compiler_params=_params("parallel", "parallel"),
        name="attn_out",
    )(sink, qb, kb2, kb2, kb2, vb2, vb2, vb2, zb, bias, hf, hb, og, zg, x, mw, wo, fw)


def _tile(n, target):
    t = min(n, target)
    assert n % t == 0, (n, t)
    return t


def kernel(x, norm_w, w_in, conv_w, conv_b, gate_b, mhn_w, sink, rel_bias, w_out, final_norm_w):
    bsz, seqlen, d_model = x.shape
    depth = norm_w.shape[0]
    assert d_model == D_MODEL and seqlen % CHUNK == 0
    tm = GROUP_CHUNKS * CHUNK
    assert seqlen % tm == 0
    n_qb = _tile(seqlen // BLOCK, 4)

    c_qk = 2 * A_WIDTH
    c_a = c_qk + 3 * A_WIDTH
    c_g = c_a + N_GATES
    assert c_a == 5 * A_WIDTH
    w_bf = w_in.astype(BF16)
    wgt = jnp.swapaxes(w_bf[:, :, c_a:c_g], 1, 2)
    wb = w_bf[:, :, c_g:]
    wo = w_out.astype(BF16)
    cw = jnp.pad(conv_w, ((0, 0), (0, SUBLANES - CONV_K), (0, 0)))
    nw = norm_w.reshape(depth, 1, D_MODEL)
    cb = conv_b.reshape(depth, 1, 2 * A_WIDTH)
    gb = gate_b.reshape(depth, N_GATES, 1)
    mw = mhn_w.reshape(depth, 1, A_WIDTH)
    bias = _bias_table(rel_bias)
    fw = final_norm_w.reshape(1, D_MODEL)

    xf = x
    for l in range(depth):
        q, kt, va, og, zg, qb, kb2, vb2, zb, ar, br, cc, bc = _in_proj(
            xf, nw, w_bf, wgt, wb, cw, cb, gb, tm=tm, layer=l)
        hf, hb = _mlstm(q, kt, va, ar, br, cc, bc)
        xf = _attn_out(sink, qb, kb2, vb2, zb, bias, hf, hb, og, zg, xf, mw, wo, fw,
                       n_qb=n_qb, layer=l, final=(l == depth - 1))
    return xf
```

```python
import functools
import math

import jax
import jax.numpy as jnp
import numpy as np
from jax import lax
from jax.experimental import pallas as pl
from jax.experimental.pallas import tpu as pltpu

D_MODEL = 1024
A_WIDTH = 512
A_HEADS = 4
A_HEAD_DIM = 128
CHUNK = 128
CONV_K = 5
B_WIDTH = 512
B_HEAD_DIM = 64
B_Q_HEADS = 8
B_KV_HEADS = 2
WINDOW = 128
BLOCK = 128
N_BUCKETS = 32
MAX_DISTANCE = 128
EPS = 1e-6
NEG_INF = -1e30
LOG2E = math.log2(math.e)
N_GATES = 4 * A_HEADS
N_CHAN = 2 * A_HEADS

LANES = 128
SUBLANES = 8
VMEM_LIMIT_BYTES = 56 * 1024 * 1024

HALO = 2 * SUBLANES
BF16 = jnp.bfloat16
F32 = jnp.float32


def _params(*sem):
    return pltpu.CompilerParams(dimension_semantics=sem, vmem_limit_bytes=VMEM_LIMIT_BYTES)


def _dot(a, b):
    return jnp.dot(a, b, preferred_element_type=F32)


def _dot_nt(a, b):
    return lax.dot_general(a, b, (((1,), (1,)), ((), ())), preferred_element_type=F32)


def _log_sigmoid(x):
    return -(jnp.maximum(-x, 0.0) + jnp.log1p(jnp.exp(-jnp.abs(x))))


def _sigmoid(x):
    return 0.5 * jnp.tanh(0.5 * x) + 0.5


def _silu_of_half(h):
    return h + h * jnp.tanh(h)


def _layer_spec(arr, layer):
    return pl.BlockSpec((1,) + arr.shape[1:], lambda *_: (layer,) + (0,) * (arr.ndim - 1))


def _in_proj_kernel(x_ref, xp_ref, xn_ref, nw_ref, wqk_ref, wv_ref, wo_ref, wz_ref, wgt_ref, wb_ref,
                    cw_ref, cb_ref, gb_ref,
                    q_ref, kt_ref, va_ref, og_ref, zg_ref, qb_ref, kb2_ref, vb2_ref, zb_ref,
                    ar_ref, br_ref, cpk_ref, bpk_ref, u_scr, *, tm):
    i = pl.program_id(1)
    last = pl.num_programs(1) - 1
    nw = nw_ref[0]

    def norm(xv):
        y = xv * lax.rsqrt(jnp.mean(xv * xv, axis=-1, keepdims=True) + EPS)
        return (y * nw).astype(BF16)

    hn = norm(x_ref[0])
    g = _dot_nt(wgt_ref[0], hn) + gb_ref[0]
    row = lax.broadcasted_iota(jnp.int32, g.shape, 0)
    gates_act = jnp.where(row < N_CHAN, g, _log_sigmoid(g))
    hp = norm(xp_ref[0])
    hx = norm(xn_ref[0])
    u_all = _dot(jnp.concatenate([hp, hn, hx], axis=0), wqk_ref[0])
    u_top = jnp.where(i == 0, 0.0, u_all[:HALO])
    u_bot = jnp.where(i == last, 0.0, u_all[HALO + tm:])
    cw_half = 0.5 * cw_ref[0]
    cb_half = 0.5 * cb_ref[0]
    parts = []
    pad = CONV_K // 2
    for c in range(2 * A_WIDTH // LANES):
        cs = slice(c * LANES, (c + 1) * LANES)
        u_scr[c, :HALO, :] = u_top[:, cs]
        u_scr[c, HALO:HALO + tm, :] = u_all[HALO:HALO + tm, cs]
        u_scr[c, HALO + tm:, :] = u_bot[:, cs]
        acc = cb_half[:, cs]
        for tap in range(CONV_K):
            lo = HALO - pad + tap
            acc = acc + u_scr[c, lo:lo + tm, :] * cw_half[tap:tap + 1, cs]
        parts.append(acc)
    qk = _silu_of_half(jnp.concatenate(parts, axis=1))
    q_ref[0] = qk[:, :A_WIDTH].astype(BF16)
    k = qk[:, A_WIDTH:] * (A_HEAD_DIM ** -0.5)
    kt_ref[0] = k.T.astype(BF16)

    va_ref[0] = _dot(hn, wv_ref[0]).astype(BF16)
    og_ref[0] = _sigmoid(_dot(hn, wo_ref[0])).astype(BF16)
    zg_ref[0] = _silu_of_half(0.5 * _dot(hn, wz_ref[0])).astype(BF16)

    _gate_scan(gates_act, ar_ref, br_ref, cpk_ref, bpk_ref, tm)

    bq = _dot(hn, wb_ref[0, :, :B_WIDTH])
    qb_ref[0] = (bq * (B_HEAD_DIM ** -0.5 * LOG2E)).astype(BF16)
    kv = _dot(hn, wb_ref[0, :, B_WIDTH:B_WIDTH + 2 * LANES])
    half = lax.broadcasted_iota(jnp.int32, (tm, LANES), 1) < B_HEAD_DIM
    for src, dst in ((kv[:, :LANES], kb2_ref), (kv[:, LANES:], vb2_ref)):
        sw = pltpu.roll(src, B_HEAD_DIM, 1)
        dst[0, :, :LANES] = jnp.where(half, src, sw).astype(BF16)
        dst[0, :, LANES:] = jnp.where(half, sw, src).astype(BF16)
    zb_ref[0] = _silu_of_half(0.5 * _dot(hn, wb_ref[0, :, B_WIDTH + 2 * LANES:])).astype(BF16)


def _in_proj(x, nw, w_all, wgt, wb, cw, cb, gb, *, tm, layer):
    bsz, seqlen, _ = x.shape
    nt = seqlen // tm
    hb = tm // HALO
    nhb = seqlen // HALO

    def full(arr):
        return _layer_spec(arr, layer)

    def w_cols(start, width):
        assert start % width == 0 and width % LANES == 0
        return pl.BlockSpec((1, D_MODEL, width), lambda b, i: (layer, 0, start // width))

    def rows(width):
        return pl.BlockSpec((1, tm, width), lambda b, i: (b, i, 0))

    out_shape = (
        jax.ShapeDtypeStruct((bsz, seqlen, A_WIDTH), BF16),
        jax.ShapeDtypeStruct((bsz, A_WIDTH, seqlen), BF16),
        jax.ShapeDtypeStruct((bsz, seqlen, A_WIDTH), BF16),
        jax.ShapeDtypeStruct((bsz, seqlen, A_WIDTH), BF16),
        jax.ShapeDtypeStruct((bsz, seqlen, A_WIDTH), BF16),
        jax.ShapeDtypeStruct((bsz, seqlen, B_WIDTH), BF16),
        jax.ShapeDtypeStruct((bsz, seqlen, 2 * LANES), BF16),
        jax.ShapeDtypeStruct((bsz, seqlen, 2 * LANES), BF16),
        jax.ShapeDtypeStruct((bsz, seqlen, B_WIDTH), BF16),
        jax.ShapeDtypeStruct((bsz, N_CHAN, seqlen), F32),
        jax.ShapeDtypeStruct((bsz, N_CHAN, seqlen), F32),
        jax.ShapeDtypeStruct((bsz, nt, CHUNK, LANES), F32),
        jax.ShapeDtypeStruct((bsz, nt, CHUNK, LANES), F32),
    )
    out_specs = (
        rows(A_WIDTH),
        pl.BlockSpec((1, A_WIDTH, tm), lambda b, i: (b, 0, i)),
        rows(A_WIDTH), rows(A_WIDTH), rows(A_WIDTH), rows(B_WIDTH),
        rows(2 * LANES), rows(2 * LANES), rows(B_WIDTH),
        pl.BlockSpec((1, N_CHAN, tm), lambda b, i: (b, 0, i)),
        pl.BlockSpec((1, N_CHAN, tm), lambda b, i: (b, 0, i)),
        pl.BlockSpec((1, 1, CHUNK, LANES), lambda b, i: (b, i, 0, 0)),
        pl.BlockSpec((1, 1, CHUNK, LANES), lambda b, i: (b, i, 0, 0)),
    )
    in_specs = [
        rows(D_MODEL),
        pl.BlockSpec((1, HALO, D_MODEL), lambda b, i: (b, jnp.maximum(i * hb - 1, 0), 0)),
        pl.BlockSpec((1, HALO, D_MODEL), lambda b, i: (b, jnp.minimum((i + 1) * hb, nhb - 1), 0)),
        full(nw),
        w_cols(0, 2 * A_WIDTH),
        w_cols(2 * A_WIDTH, A_WIDTH),
        w_cols(3 * A_WIDTH, A_WIDTH),
        w_cols(4 * A_WIDTH, A_WIDTH),
        full(wgt), full(wb), full(cw), full(cb), full(gb),
    ]
    return pl.pallas_call(
        functools.partial(_in_proj_kernel, tm=tm),
        grid=(bsz, nt),
        in_specs=in_specs,
        out_specs=out_specs,
        out_shape=out_shape,
        scratch_shapes=[pltpu.VMEM((2 * A_WIDTH // LANES, tm + 2 * HALO, LANES), F32)],
        compiler_params=_params("parallel", "arbitrary"),
        name="in_proj",
    )(x, x, x, nw, w_all, w_all, w_all, w_all, wgt, wb, cw, cb, gb)


GROUP_CHUNKS = 8


def _gate_scan(gates, ar_ref, br_ref, cpk_ref, bpk_ref, tm):
    n_chunks = tm // CHUNK
    n_rows = n_chunks * N_CHAN
    ti = lax.broadcasted_iota(jnp.int32, (CHUNK, CHUNK), 0)
    si = lax.broadcasted_iota(jnp.int32, (CHUNK, CHUNK), 1)
    upper_f = (ti <= si).astype(F32)
    lower_f = (ti >= si).astype(F32)
    ri = lax.broadcasted_iota(jnp.int32, (n_rows, CHUNK), 0)
    lane = lax.broadcasted_iota(jnp.int32, (n_rows, CHUNK), 1)
    fwd_row = lax.rem(ri, N_CHAN) < A_HEADS
    li = jnp.concatenate([gates[:N_CHAN, c * CHUNK:(c + 1) * CHUNK]
                          for c in range(n_chunks)], axis=0) * LOG2E
    lf = jnp.concatenate([gates[N_CHAN:, c * CHUNK:(c + 1) * CHUNK]
                          for c in range(n_chunks)], axis=0) * LOG2E
    b_pre = jnp.dot(lf, upper_f, preferred_element_type=F32, precision=lax.Precision.HIGHEST)
    b_suf = jnp.dot(lf, lower_f, preferred_element_type=F32, precision=lax.Precision.HIGHEST)
    b = jnp.where(fwd_row, b_pre, b_suf)
    a = li - b
    pre, suf = a, a
    sh = 1
    while sh < CHUNK:
        pre = jnp.where(lane >= sh, jnp.maximum(pre, pltpu.roll(pre, sh, 1)), pre)
        suf = jnp.where(lane < CHUNK - sh, jnp.maximum(suf, pltpu.roll(suf, CHUNK - sh, 1)), suf)
        sh *= 2
    cmax = jnp.where(fwd_row, pre, suf)
    for c in range(n_chunks):
        rs = slice(c * N_CHAN, (c + 1) * N_CHAN)
        ar_ref[0, :, c * CHUNK:(c + 1) * CHUNK] = a[rs, :]
        br_ref[0, :, c * CHUNK:(c + 1) * CHUNK] = b[rs, :]
    zpad = jnp.zeros((LANES - n_rows, CHUNK), F32)
    cpk_ref[0, 0] = jnp.concatenate([cmax, zpad], axis=0).T
    bpk_ref[0, 0] = jnp.concatenate([b, zpad], axis=0).T


CHUNKS_PER_STEP = 4


def _mlstm_kernel(qf_ref, qb_ref, ktf_ref, ktb_ref, vf_ref, vb_ref,
                  arf_ref, arb_ref, brf_ref, brb_ref, ccf_ref, ccb_ref, bcf_ref, bcb_ref,
                  hf_ref, hb_ref, c_scr, m_scr, *, bsz):
    j = pl.program_id(0)

    @pl.when(j == 0)
    def _():
        c_scr[...] = jnp.zeros_like(c_scr)
        m_scr[...] = jnp.zeros_like(m_scr)

    ti = lax.broadcasted_iota(jnp.int32, (CHUNK, CHUNK), 0)
    si = lax.broadcasted_iota(jnp.int32, (CHUNK, CHUNK), 1)
    ones_blk = jnp.ones((CHUNK, A_HEAD_DIM), BF16)
    sub8 = lax.broadcasted_iota(jnp.int32, (N_CHAN, LANES), 0)
    lane8 = lax.broadcasted_iota(jnp.int32, (N_CHAN, LANES), 1)

    n_steps = pl.num_programs(0)
    for k in range(CHUNKS_PER_STEP):
        _mlstm_chunk(j, k, n_steps, qf_ref, qb_ref, ktf_ref, ktb_ref, vf_ref, vb_ref,
                     arf_ref, arb_ref, brf_ref, brb_ref, ccf_ref, ccb_ref, bcf_ref, bcb_ref,
                     hf_ref, hb_ref, c_scr, m_scr, bsz, ti, si, ones_blk, sub8, lane8)


def _mlstm_chunk(j, k, n_steps, qf_ref, qb_ref, ktf_ref, ktb_ref, vf_ref, vb_ref,
                 arf_ref, arb_ref, brf_ref, brb_ref, ccf_ref, ccb_ref, bcf_ref, bcb_ref,
                 hf_ref, hb_ref, c_scr, m_scr, bsz, ti, si, ones_blk, sub8, lane8):
    jf = j * CHUNKS_PER_STEP + k
    jb = n_steps * CHUNKS_PER_STEP - 1 - jf
    base_f = lax.rem(jf, GROUP_CHUNKS) * N_CHAN
    base_b = lax.rem(jb, GROUP_CHUNKS) * N_CHAN
    rows_f = slice(k * CHUNK, (k + 1) * CHUNK)
    rows_b = slice((CHUNKS_PER_STEP - 1 - k) * CHUNK, (CHUNKS_PER_STEP - k) * CHUNK)
    dirs = (
        (0, qf_ref, ktf_ref, vf_ref, arf_ref, brf_ref, ccf_ref, bcf_ref, hf_ref, ti >= si, CHUNK - 1,
         base_f, rows_f),
        (1, qb_ref, ktb_ref, vb_ref, arb_ref, brb_ref, ccb_ref, bcb_ref, hb_ref, ti <= si, 0,
         base_b, rows_b),
    )

    def body(b, carry):
        tiles = []
        for d, q_ref, kt_ref, v_ref, ar_ref, br_ref, cc_ref, bc_ref, h_ref, vis, last, base, rows in dirs:
            for h in range(A_HEADS):
                hs = slice(h * A_HEAD_DIM, (h + 1) * A_HEAD_DIM)
                idx = (b * 2 + d) * A_HEADS + h
                q = q_ref[b, rows, hs]
                kt = kt_ref[b, hs, rows]
                tiles.append((_dot(q, kt), q, kt, idx, hs))

        gates = []
        for d, q_ref, kt_ref, v_ref, ar_ref, br_ref, cc_ref, bc_ref, h_ref, vis, last, base, rows in dirs:
            a_row = ar_ref[b, :, rows]
            b_last = br_ref[b, :, rows][:, last:last + 1]
            m8 = m_scr[b * 2 + d]
            gl8 = jnp.maximum(m8, jnp.max(a_row, axis=1, keepdims=True))
            w_row = jnp.exp2(a_row - gl8)
            decay8 = jnp.exp2(m8 - gl8)
            m_scr[b * 2 + d] = b_last + gl8
            m_lane = jnp.sum(jnp.where(sub8 == lane8, m8, 0.0), axis=0, keepdims=True)
            unrot = lax.rem(LANES - base, LANES)
            g = jnp.maximum(m_lane, pltpu.roll(cc_ref[b, 0], unrot, 1))
            emt = jnp.exp2(-(pltpu.roll(bc_ref[b, 0], unrot, 1) + g))
            gates.append((a_row, w_row, decay8, m8, g, emt))

        for d, q_ref, kt_ref, v_ref, ar_ref, br_ref, cc_ref, bc_ref, h_ref, vis, last, base, rows in dirs:
            a_row, w_row, decay8, m8, g, emt = gates[d]
            for h in range(A_HEADS):
                ch = d * A_HEADS + h
                qk, q, kt, idx, hs = tiles[ch]
                c_old = c_scr[idx]
                v_aug = jnp.concatenate([v_ref[b, rows, hs], ones_blk], axis=1)
                g_b = jnp.broadcast_to(g[:, ch:ch + 1], (CHUNK, LANES))
                p = jnp.where(vis, jnp.exp2(a_row[ch:ch + 1, :] - g_b), 0.0)
                iw_b = jnp.exp2(m8[ch:ch + 1, :] - g_b)
                lhs = jnp.concatenate([(qk * p).astype(BF16),
                                       (q.astype(F32) * iw_b).astype(BF16)], axis=1)
                rhs = jnp.concatenate([v_aug, c_old.astype(BF16)], axis=0)
                r = _dot(lhs, rhs)
                num = r[:, :A_HEAD_DIM]
                den = r[:, A_HEAD_DIM:]
                h_ref[b, rows, hs] = (num / jnp.maximum(jnp.abs(den), emt[:, ch:ch + 1])).astype(BF16)
                ktw = (kt.astype(F32) * w_row[ch:ch + 1, :]).astype(BF16)
                c_scr[idx] = decay8[ch:ch + 1, :1] * c_old + _dot(ktw, v_aug)
        return carry

    lax.fori_loop(0, bsz, body, 0, unroll=True)


def _mlstm(q, kt, va, ar, br, cc, bc):
    bsz, seqlen, _ = q.shape
    nc = seqlen // CHUNK
    cps = CHUNKS_PER_STEP
    assert nc % cps == 0 and GROUP_CHUNKS % cps == 0
    ns = nc // cps
    fwd3 = lambda j: (0, j, 0)
    bwd3 = lambda j: (0, ns - 1 - j, 0)
    fwd3t = lambda j: (0, 0, j)
    bwd3t = lambda j: (0, 0, ns - 1 - j)
    tok = (bsz, cps * CHUNK, A_WIDTH)
    tok_t = (bsz, A_WIDTH, cps * CHUNK)
    rowb = (bsz, N_CHAN, cps * CHUNK)
    colb = (bsz, 1, CHUNK, LANES)
    fwd4 = lambda j: (0, (j * cps) // GROUP_CHUNKS, 0, 0)
    bwd4 = lambda j: (0, (nc - 1 - j * cps) // GROUP_CHUNKS, 0, 0)
    in_specs = [
        pl.BlockSpec(tok, fwd3), pl.BlockSpec(tok, bwd3),
        pl.BlockSpec(tok_t, fwd3t), pl.BlockSpec(tok_t, bwd3t),
        pl.BlockSpec(tok, fwd3), pl.BlockSpec(tok, bwd3),
        pl.BlockSpec(rowb, fwd3t), pl.BlockSpec(rowb, bwd3t),
        pl.BlockSpec(rowb, fwd3t), pl.BlockSpec(rowb, bwd3t),
        pl.BlockSpec(colb, fwd4), pl.BlockSpec(colb, bwd4),
        pl.BlockSpec(colb, fwd4), pl.BlockSpec(colb, bwd4),
    ]
    return pl.pallas_call(
        functools.partial(_mlstm_kernel, bsz=bsz),
        grid=(ns,),
        in_specs=in_specs,
        out_specs=(pl.BlockSpec(tok, fwd3), pl.BlockSpec(tok, bwd3)),
        out_shape=(jax.ShapeDtypeStruct((bsz, seqlen, A_WIDTH), BF16),
                   jax.ShapeDtypeStruct((bsz, seqlen, A_WIDTH), BF16)),
        scratch_shapes=[pltpu.VMEM((bsz * 2 * A_HEADS, A_HEAD_DIM, 2 * A_HEAD_DIM), F32),
                        pltpu.VMEM((bsz * 2, N_CHAN, LANES), F32)],
        compiler_params=_params("arbitrary"),
        name="mlstm",
    )(q, q, kt, kt, va, va, ar, ar, br, br, cc, cc, bc, bc)


def _t5_bucket(rel):
    nb = N_BUCKETS // 2
    max_exact = nb // 2
    ret = jnp.where(rel > 0, nb, 0)
    n = jnp.abs(rel)
    nf = jnp.maximum(n, 1).astype(jnp.float32)
    large = max_exact + (jnp.log(nf / max_exact) / math.log(MAX_DISTANCE / max_exact)
                         * (nb - max_exact)).astype(jnp.int32)
    large = jnp.minimum(large, nb - 1)
    return ret + jnp.where(n < max_exact, n, large)


def _bias_kernel(rb_ref, bucket_ref, bias_ref):
    bucket = bucket_ref[...]
    qi = lax.broadcasted_iota(jnp.int32, bucket.shape, 0)
    kj = lax.broadcasted_iota(jnp.int32, bucket.shape, 1)
    band = jnp.abs(kj - BLOCK - qi) <= WINDOW
    masks = (band & (kj >= BLOCK), band, band & (kj < 2 * BLOCK))
    for hq in range(B_Q_HEADS):
        acc = jnp.zeros(bucket.shape, F32)
        for nb in range(N_BUCKETS):
            acc = jnp.where(bucket == nb, rb_ref[nb, hq], acc)
        acc = acc * LOG2E
        for v, mask in enumerate(masks):
            bias_ref[v, hq] = jnp.where(mask, acc, NEG_INF)


def _bias_table(rel_bias):
    q_off = jnp.arange(BLOCK)
    k_off = jnp.arange(3 * BLOCK) - BLOCK
    bucket = _t5_bucket(k_off[None, :] - q_off[:, None]).astype(jnp.int32)
    shape = (3, B_Q_HEADS, BLOCK, 3 * BLOCK)
    return pl.pallas_call(
        _bias_kernel,
        in_specs=[pl.BlockSpec(memory_space=pltpu.SMEM),
                  pl.BlockSpec(bucket.shape, lambda: (0, 0))],
        out_specs=pl.BlockSpec(shape, lambda: (0, 0, 0, 0)),
        out_shape=jax.ShapeDtypeStruct(shape, F32),
        name="bias_table",
    )(rel_bias.astype(F32), bucket)


def _attn_out_kernel(sink_ref, q_ref, kp_ref, kc_ref, kn_ref, vp_ref, vc_ref, vn_ref, zb_ref,
                     bias_ref, hf_ref, hb_ref, og_ref, zg_ref, x_ref, mw_ref, wo_ref, fw_ref,
                     o_ref, s_scr, yb_scr, *, n_qb, layer, final):
    j = pl.program_id(1)
    last = pl.num_programs(1) - 1
    n_keys = (n_qb + 2) * BLOCK
    lo_k = lax.broadcasted_iota(jnp.int32, (n_keys, LANES), 1) < B_HEAD_DIM
    lo_q = lax.broadcasted_iota(jnp.int32, (BLOCK, LANES), 1) < B_HEAD_DIM
    zero = jnp.zeros((n_keys, LANES), BF16)
    k_sel, v_sel = [], []
    for h in range(B_KV_HEADS):
        hs = slice(h * LANES, (h + 1) * LANES)
        k2 = jnp.concatenate([kp_ref[0, :, hs], kc_ref[0, :, hs], kn_ref[0, :, hs]], axis=0)
        v2 = jnp.concatenate([vp_ref[0, :, hs], vc_ref[0, :, hs], vn_ref[0, :, hs]], axis=0)
        k_sel.append((jnp.where(lo_k, k2, zero), jnp.where(lo_k, zero, k2)))
        v_sel.append((jnp.where(lo_k, v2, zero), jnp.where(lo_k, zero, v2)))

    for i in range(n_qb):
        variant = jnp.int32(1)
        if i == 0:
            variant = jnp.where(j == 0, 0, variant)
        if i == n_qb - 1:
            variant = jnp.where(j == last, 2, variant)
        rows = slice(i * BLOCK, (i + 1) * BLOCK)
        win = slice(i * BLOCK, (i + 3) * BLOCK)
        for p_idx in range(B_Q_HEADS // 2):
            qp = q_ref[0, rows, p_idx * LANES:(p_idx + 1) * LANES]
            for par in range(2):
                hq = p_idx * 2 + par
                k_win = k_sel[p_idx // 2][par][win]
                s_scr[i * B_Q_HEADS + hq] = _dot_nt(qp, k_win) + bias_ref[variant, hq]

    for i in range(n_qb):
        rows = slice(i * BLOCK, (i + 1) * BLOCK)
        win = slice(i * BLOCK, (i + 3) * BLOCK)
        for p_idx in range(B_Q_HEADS // 2):
            ps = slice(p_idx * LANES, (p_idx + 1) * LANES)
            probs, dens = [], []
            for par in range(2):
                hq = p_idx * 2 + par
                sink = sink_ref[layer, hq] * LOG2E
                sc = s_scr[i * B_Q_HEADS + hq]
                m = jnp.maximum(jnp.max(sc, axis=-1, keepdims=True), sink)
                p = jnp.exp2(sc - m)
                dens.append(jnp.sum(p, axis=-1, keepdims=True) + jnp.exp2(sink - m))
                probs.append(p.astype(BF16))
            v_even, v_odd = v_sel[p_idx // 2]
            v_bd = jnp.concatenate([v_even[win], v_odd[win]], axis=0)
            out = _dot(jnp.concatenate(probs, axis=1), v_bd)
            y = out / jnp.where(lo_q, dens[0], dens[1])
            yb_scr[rows, ps] = (y * zb_ref[0, rows, ps].astype(F32)).astype(BF16)

    h = og_ref[0].astype(F32) * (hf_ref[0].astype(F32) + hb_ref[0].astype(F32))
    parts = []
    for k in range(A_HEADS):
        hs = slice(k * A_HEAD_DIM, (k + 1) * A_HEAD_DIM)
        hh = h[:, hs]
        hh = hh * lax.rsqrt(jnp.mean(hh * hh, axis=-1, keepdims=True) + EPS)
        parts.append(hh * mw_ref[0, :, hs])
    ya = (jnp.concatenate(parts, axis=1) * zg_ref[0].astype(F32)).astype(BF16)
    out = x_ref[0] + _dot(ya, wo_ref[0, :A_WIDTH, :]) + _dot(yb_scr[...], wo_ref[0, A_WIDTH:, :])
    if final:
        out = out * lax.rsqrt(jnp.mean(out * out, axis=-1, keepdims=True) + EPS) * fw_ref[...]
    o_ref[0] = out


def _attn_out(sink, qb, kb2, vb2, zb, bias, hf, hb, og, zg, x, mw, wo, fw, *, n_qb, layer, final):
    bsz, seqlen, _ = qb.shape
    nb = seqlen // BLOCK
    assert nb >= 2 and nb % n_qb == 0
    tq = n_qb * BLOCK
    cur = lambda b, j: (b, j, 0)
    prev = lambda b, j: (b, jnp.maximum(j * n_qb - 1, 0), 0)
    nxt = lambda b, j: (b, jnp.minimum((j + 1) * n_qb, nb - 1), 0)
    halo = pl.BlockSpec((1, BLOCK, 2 * LANES), prev), pl.BlockSpec((1, BLOCK, 2 * LANES), nxt)
    kv_cur = pl.BlockSpec((1, tq, 2 * LANES), cur)

    def tok(width):
        return pl.BlockSpec((1, tq, width), cur)

    def full(arr):
        return pl.BlockSpec(arr.shape, lambda b, j: (0,) * arr.ndim)

    return pl.pallas_call(
        functools.partial(_attn_out_kernel, n_qb=n_qb, layer=layer, final=final),
        grid=(bsz, nb // n_qb),
        in_specs=[pl.BlockSpec(memory_space=pltpu.SMEM),
                  tok(B_WIDTH),
                  halo[0], kv_cur, halo[1],
                  halo[0], kv_cur, halo[1],
                  tok(B_WIDTH), full(bias),
                  tok(A_WIDTH), tok(A_WIDTH), tok(A_WIDTH), tok(A_WIDTH), tok(D_MODEL),
                  _layer_spec(mw, layer), _layer_spec(wo, layer), full(fw)],
        out_specs=tok(D_MODEL),
        out_shape=jax.ShapeDtypeStruct(x.shape, F32),
        scratch_shapes=[pltpu.VMEM((n_qb * B_Q_HEADS, BLOCK, 3 * BLOCK), F32),
                        pltpu.VMEM((tq, B_WIDTH), BF16)],
        compiler_params=_params("parallel", "parallel"),
        name="attn_out",
    )(sink, qb, kb2, kb2, kb2, vb2, vb2, vb2, zb, bias, hf, hb, og, zg, x, mw, wo, fw)


def _tile(n, target):
    t = min(n, target)
    assert n % t == 0, (n, t)
    return t


def kernel(x, norm_w, w_in, conv_w, conv_b, gate_b, mhn_w, sink, rel_bias, w_out, final_norm_w):
    bsz, seqlen, d_model = x.shape
    depth = norm_w.shape[0]
    assert d_model == D_MODEL and seqlen % CHUNK == 0
    tm = GROUP_CHUNKS * CHUNK
    assert seqlen % tm == 0
    n_qb = _tile(seqlen // BLOCK, 4)

    c_qk = 2 * A_WIDTH
    c_a = c_qk + 3 * A_WIDTH
    c_g = c_a + N_GATES
    assert c_a == 5 * A_WIDTH
    w_bf = w_in.astype(BF16)
    wgt = jnp.swapaxes(w_bf[:, :, c_a:c_g], 1, 2)
    wb = w_bf[:, :, c_g:]
    wo = w_out.astype(BF16)
    cw = jnp.pad(conv_w, ((0, 0), (0, SUBLANES - CONV_K), (0, 0)))
    nw = norm_w.reshape(depth, 1, D_MODEL)
    cb = conv_b.reshape(depth, 1, 2 * A_WIDTH)
    gb = gate_b.reshape(depth, N_GATES, 1)
    mw = mhn_w.reshape(depth, 1, A_WIDTH)
    bias = _bias_table(rel_bias)
    fw = final_norm_w.reshape(1, D_MODEL)

    xf = x
    for l in range(depth):
        q, kt, va, og, zg, qb, kb2, vb2, zb, ar, br, cc, bc = _in_proj(
            xf, nw, w_bf, wgt, wb, cw, cb, gb, tm=tm, layer=l)
        hf, hb = _mlstm(q, kt, va, ar, br, cc, bc)
        xf = _attn_out(sink, qb, kb2, vb2, zb, bias, hf, hb, og, zg, xf, mw, wo, fw,
                       n_qb=n_qb, layer=l, final=(l == depth - 1))
    return xf
```
